```python
import jax, jax.numpy as jnp
from jax import lax
import numpy as np

D_MODEL = 1024
BATCH = 2
SEQ = 16384
DEPTH = 2

MLSTM_HEADS = 4
MLSTM_QK_DIM = 64
MLSTM_V_DIM = 128
MLSTM_QK_W = MLSTM_HEADS * MLSTM_QK_DIM
MLSTM_W = MLSTM_HEADS * MLSTM_V_DIM
MLSTM_CHUNK = 128
QK_CONV = 4
FORGET_BIAS = 3.0
ATTN_HEADS = 8
ATTN_KV_HEADS = 2
ATTN_HEAD_DIM = 64
ATTN_W = ATTN_HEADS * ATTN_HEAD_DIM
ATTN_KV_W = ATTN_KV_HEADS * ATTN_HEAD_DIM
WINDOW = 128
ROPE_THETA = 10000.0
MIX_W = MLSTM_W + ATTN_W
SPLITS = (MLSTM_QK_W, MLSTM_QK_W, MLSTM_W, MLSTM_W, 2 * MLSTM_HEADS, ATTN_W, ATTN_KV_W, ATTN_KV_W)
IN_W = sum(SPLITS)
D_FF = 2816
FFN_CONV = 3
EPS = 1e-6

kernel_name = "hymba_mlstm_swa_sink_convffn"


def rmsnorm(x, g):
    xf = x.astype(jnp.float32)
    y = xf * lax.rsqrt(jnp.mean(xf * xf, axis=-1, keepdims=True) + EPS)
    return (y * g.astype(jnp.float32)).astype(x.dtype)


def causal_dwconv(x, w):
    K, C = w.shape
    return lax.conv_general_dilated(
        x, w[:, None, :].astype(x.dtype), window_strides=(1,), padding=[(K - 1, 0)],
        dimension_numbers=('NWC', 'WIO', 'NWC'), feature_group_count=C)


def rope(x):
    S, D = x.shape[1], x.shape[-1]
    inv = 1.0 / (ROPE_THETA ** (jnp.arange(0, D, 2, dtype=jnp.float32) / D))
    ang = jnp.arange(S, dtype=jnp.float32)[:, None] * inv[None, :]
    cos = jnp.cos(ang)[None, :, None, :]
    sin = jnp.sin(ang)[None, :, None, :]
    xf = x.astype(jnp.float32)
    x1, x2 = xf[..., :D // 2], xf[..., D // 2:]
    return jnp.concatenate([x1 * cos - x2 * sin, x2 * cos + x1 * sin], axis=-1)


def mlstm_chunkwise(q, k, v, log_i, log_f):
    B, S, H, DK = q.shape
    DV = v.shape[-1]
    L = MLSTM_CHUNK
    NC = S // L

    def chunks(a):
        a = a.reshape((B, NC, L) + a.shape[2:])
        return jnp.moveaxis(a, (1, 3), (0, 2))

    qc = chunks(q)
    kc = chunks(k * (DK ** -0.5))
    vc = chunks(v)
    lic = chunks(log_i)
    bc = jnp.cumsum(chunks(log_f), axis=-1)
    causal = jnp.tril(jnp.ones((L, L), dtype=bool))

    def step(carry, inp):
        C, n, m = carry
        q_, k_, v_, b_, li_ = inp
        dlog = b_[..., :, None] - b_[..., None, :] + li_[..., None, :]
        dlog = jnp.where(causal, dlog, -jnp.inf)
        inter = b_ + m[..., None]
        m_row = jnp.maximum(inter, jnp.max(dlog, axis=-1))
        w_intra = jnp.exp(dlog - m_row[..., None])
        w_inter = jnp.exp(inter - m_row)
        s = jnp.einsum('bhtd,bhsd->bhts', q_, k_) * w_intra
        num = (w_inter[..., None] * jnp.einsum('bhtd,bhde->bhte', q_, C)
               + jnp.einsum('bhts,bhse->bhte', s, v_))
        den = w_inter * jnp.einsum('bhtd,bhd->bht', q_, n) + jnp.sum(s, axis=-1)
        h = num / jnp.maximum(jnp.abs(den), jnp.exp(-m_row))[..., None]
        b_last = b_[..., -1]
        ls = b_last[..., None] - b_ + li_
        m_new = jnp.maximum(b_last + m, jnp.max(ls, axis=-1))
        decay = jnp.exp(b_last + m - m_new)
        ws = jnp.exp(ls - m_new[..., None])
        C = decay[..., None, None] * C + jnp.einsum('bhs,bhsd,bhse->bhde', ws, k_, v_)
        n = decay[..., None] * n + jnp.einsum('bhs,bhsd->bhd', ws, k_)
        return (C, n, m_new), h

    init = (jnp.zeros((B, H, DK, DV), jnp.float32), jnp.zeros((B, H, DK), jnp.float32),
            jnp.zeros((B, H), jnp.float32))
    _, hs = lax.scan(step, init, (qc, kc, vc, bc, lic))
    return jnp.moveaxis(hs, (0, 2), (1, 3)).reshape(B, S, H, DV)


def sliding_window_attention(q, k, v, sinks):
    B, S, H, D = q.shape
    KV = k.shape[2]
    G = H // KV
    W = WINDOW
    NB = S // W
    qb = q.reshape(B, NB, W, KV, G, D)
    kb = k.reshape(B, NB, W, KV, D)
    vb = v.reshape(B, NB, W, KV, D)
    kk = jnp.concatenate([jnp.concatenate([jnp.zeros_like(kb[:, :1]), kb[:, :-1]], axis=1), kb], axis=2)
    vv = jnp.concatenate([jnp.concatenate([jnp.zeros_like(vb[:, :1]), vb[:, :-1]], axis=1), vb], axis=2)
    s = jnp.einsum('bnqkgd,bnskd->bnkgqs', qb, kk) * (D ** -0.5)
    qpos = jnp.arange(W)[:, None] + W
    kpos = jnp.arange(2 * W)[None, :]
    diff = qpos - kpos
    band = (diff >= 0) & (diff < W)
    first = (jnp.arange(NB) == 0)[:, None, None]
    valid = band[None] & ~(first & (kpos[None] < W))
    s = jnp.where(valid[None, :, None, None], s, -jnp.inf)
    sink = sinks.astype(jnp.float32).reshape(1, 1, KV, G, 1, 1)
    mx = jnp.maximum(jnp.max(s, axis=-1, keepdims=True), sink)
    p = jnp.exp(s - mx)
    denom = jnp.sum(p, axis=-1, keepdims=True) + jnp.exp(sink - mx)
    o = jnp.einsum('bnkgqs,bnskd->bnqkgd', p / denom, vv)
    return o.reshape(B, S, H * D)


def token_mixer(h, w_in, qk_conv_w, qk_conv_b, gate_bias, mh_norm_g, attn_sinks, w_out):
    B, S, _ = h.shape
    z = h @ w_in
    parts = []
    off = 0
    for w in SPLITS:
        parts.append(z[..., off:off + w])
        off += w
    mq, mk, mv, mo, mif, aq, ak, av = parts
    qk = jax.nn.silu(causal_dwconv(jnp.concatenate([mq, mk], axis=-1), qk_conv_w) + qk_conv_b)
    qk = qk.astype(jnp.float32)
    mq_h = qk[..., :MLSTM_QK_W].reshape(B, S, MLSTM_HEADS, MLSTM_QK_DIM)
    mk_h = qk[..., MLSTM_QK_W:].reshape(B, S, MLSTM_HEADS, MLSTM_QK_DIM)
    mv_h = mv.astype(jnp.float32).reshape(B, S, MLSTM_HEADS, MLSTM_V_DIM)
    gates = mif.astype(jnp.float32) + gate_bias.astype(jnp.float32)
    log_i = gates[..., :MLSTM_HEADS]
    log_f = jax.nn.log_sigmoid(gates[..., MLSTM_HEADS:])
    ht = mlstm_chunkwise(mq_h, mk_h, mv_h, log_i, log_f)
    ht = ht * lax.rsqrt(jnp.mean(ht * ht, axis=-1, keepdims=True) + EPS)
    m_out = ht.reshape(B, S, MLSTM_W) * mh_norm_g.astype(jnp.float32) * jax.nn.sigmoid(mo.astype(jnp.float32))
    q = rope(aq.reshape(B, S, ATTN_HEADS, ATTN_HEAD_DIM))
    k = rope(ak.reshape(B, S, ATTN_KV_HEADS, ATTN_HEAD_DIM))
    v = av.astype(jnp.float32).reshape(B, S, ATTN_KV_HEADS, ATTN_HEAD_DIM)
    a_out = sliding_window_attention(q, k, v, attn_sinks)
    y = jnp.concatenate([m_out, a_out], axis=-1).astype(h.dtype)
    return y @ w_out


def conv_ffn(h, w_up, ffn_conv_w, w_down):
    u = causal_dwconv(h @ w_up, ffn_conv_w)
    gate, val = u[..., :D_FF], u[..., D_FF:]
    return (jax.nn.silu(gate) * val) @ w_down


def setup_inputs(seed: int = 0) -> dict:
    key = jax.random.key(seed)
    ks = jax.random.split(key, 20)
    f32 = jnp.float32
    nrm = lambda k, shape: jax.random.normal(k, shape, f32)
    return {
        "x": nrm(ks[0], (BATCH, SEQ, D_MODEL)),
        "g_pre_mix": 1.0 + 0.05 * nrm(ks[1], (DEPTH, D_MODEL)),
        "w_in": nrm(ks[2], (DEPTH, D_MODEL, IN_W)) * D_MODEL ** -0.5,
        "qk_conv_w": nrm(ks[3], (DEPTH, QK_CONV, 2 * MLSTM_QK_W)) * QK_CONV ** -0.5,
        "qk_conv_b": 0.02 * nrm(ks[4], (DEPTH, 2 * MLSTM_QK_W)),
        "gate_bias": jnp.concatenate([0.1 * nrm(ks[5], (DEPTH, MLSTM_HEADS)),
                                      FORGET_BIAS + 0.5 * nrm(ks[6], (DEPTH, MLSTM_HEADS))], axis=-1),
        "mh_norm_g": 1.0 + 0.05 * nrm(ks[7], (DEPTH, MLSTM_W)),
        "attn_sinks": 0.5 * nrm(ks[8], (DEPTH, ATTN_HEADS)),
        "w_out": nrm(ks[9], (DEPTH, MIX_W, D_MODEL)) * MIX_W ** -0.5,
        "g_post_mix": 1.0 + 0.05 * nrm(ks[10], (DEPTH, D_MODEL)),
        "g_pre_ffn": 1.0 + 0.05 * nrm(ks[11], (DEPTH, D_MODEL)),
        "w_up": nrm(ks[12], (DEPTH, D_MODEL, 2 * D_FF)) * D_MODEL ** -0.5,
        "ffn_conv_w": nrm(ks[13], (DEPTH, FFN_CONV, 2 * D_FF)) * FFN_CONV ** -0.5,
        "w_down": nrm(ks[14], (DEPTH, D_FF, D_MODEL)) * D_FF ** -0.5,
        "g_post_ffn": 1.0 + 0.05 * nrm(ks[15], (DEPTH, D_MODEL)),
    }


def reference(x, g_pre_mix, w_in, qk_conv_w, qk_conv_b, gate_bias, mh_norm_g, attn_sinks, w_out,
              g_post_mix, g_pre_ffn, w_up, ffn_conv_w, w_down, g_post_ffn):
    for l in range(DEPTH):
        h = rmsnorm(x, g_pre_mix[l])
        y = token_mixer(h, w_in[l], qk_conv_w[l], qk_conv_b[l], gate_bias[l], mh_norm_g[l],
                        attn_sinks[l], w_out[l])
        x = x + rmsnorm(y, g_post_mix[l])
        h = rmsnorm(x, g_pre_ffn[l])
        y = conv_ffn(h, w_up[l], ffn_conv_w[l], w_down[l])
        x = x + rmsnorm(y, g_post_ffn[l])
    return x
```

```python
import functools

import jax
import jax.numpy as jnp
import numpy as np
from jax import lax
from jax.experimental import pallas as pl
from jax.experimental.pallas import tpu as pltpu

F32 = jnp.float32
BF16 = jnp.bfloat16

D_MODEL = 1024
MLSTM_HEADS = 4
MLSTM_QK_DIM = 64
MLSTM_V_DIM = 128
MLSTM_QK_W = MLSTM_HEADS * MLSTM_QK_DIM
MLSTM_W = MLSTM_HEADS * MLSTM_V_DIM
CHUNK = 128
QK_CONV = 4
ATTN_HEADS = 8
ATTN_KV_HEADS = 2
ATTN_HEAD_DIM = 64
ATTN_W = ATTN_HEADS * ATTN_HEAD_DIM
ATTN_KV_W = ATTN_KV_HEADS * ATTN_HEAD_DIM
WINDOW = 128
ROPE_THETA = 10000.0
D_FF = 2816
FFN_CONV = 3
EPS = 1e-6

LANES = 128
SUBLANES = 8
GATE_W = LANES
IN_COLS = 2 * MLSTM_QK_W + 2 * MLSTM_W + ATTN_W + 2 * ATTN_KV_W + GATE_W

TM_PROJ = 512
TB_MLSTM = 512
TQ_SWA = 512
TM_FFN = 1024
TF_FFN = 256
VMEM_LIMIT = 56 * 1024 * 1024


def _sigmoid(x):
    return 1.0 / (1.0 + jnp.exp(-x))


def _rms(x, g):
    return x * lax.rsqrt(jnp.mean(x * x, axis=-1, keepdims=True) + EPS) * g


def _split3(x):
    hi = x.astype(BF16)
    r1 = x - hi.astype(F32)
    mid = r1.astype(BF16)
    lo = (r1 - mid.astype(F32)).astype(BF16)
    return hi, mid, lo


def _inproj_kernel(x_ref, g_ref, w_ref, cos_ref, sin_ref,
                   qk_ref, mv_ref, mo_ref, gate_ref, aq_ref, ak_ref, av_ref):
    h = _rms(x_ref[...], g_ref[...]).astype(BF16)

    def proj(lo, width):
        return jnp.dot(h, w_ref[:, lo:lo + width], preferred_element_type=F32)

    off = 0
    qk_ref[...] = proj(off, 2 * MLSTM_QK_W).astype(BF16)
    off += 2 * MLSTM_QK_W
    mv_ref[...] = proj(off, MLSTM_W).astype(BF16)
    off += MLSTM_W
    mo_ref[...] = proj(off, MLSTM_W).astype(BF16)
    off += MLSTM_W

    cos = cos_ref[...]
    sin = sin_ref[...]
    lane = lax.broadcasted_iota(jnp.int32, cos.shape, 1)
    first_half = (lane % ATTN_HEAD_DIM) < (ATTN_HEAD_DIM // 2)

    def rope(t):
        partner = jnp.where(first_half,
                            pltpu.roll(t, LANES - ATTN_HEAD_DIM // 2, axis=1),
                            pltpu.roll(t, ATTN_HEAD_DIM // 2, axis=1))
        return t * cos + partner * sin

    aq = proj(off, ATTN_W)
    off += ATTN_W
    scale = ATTN_HEAD_DIM ** -0.5
    for c in range(ATTN_W // LANES):
        sl = slice(c * LANES, (c + 1) * LANES)
        aq_ref[:, sl] = (rope(aq[:, sl]) * scale).astype(BF16)
    ak_ref[...] = rope(proj(off, ATTN_KV_W)).astype(BF16)
    off += ATTN_KV_W
    av_ref[...] = proj(off, ATTN_KV_W).astype(BF16)
    off += ATTN_KV_W
    gate_ref[...] = proj(off, GATE_W)


def _inproj(x2, g, w, cos_t, sin_t, seq):
    T = x2.shape[0]
    tm = min(TM_PROJ, seq)
    nseq = seq // tm
    row = lambda i: (i, 0)
    const = lambda i: (0, 0)
    pos = lambda i: (i % nseq, 0)
    outs = [
        (2 * MLSTM_QK_W, BF16), (MLSTM_W, BF16), (MLSTM_W, BF16), (GATE_W, F32),
        (ATTN_W, BF16), (ATTN_KV_W, BF16), (ATTN_KV_W, BF16),
    ]
    return pl.pallas_call(
        _inproj_kernel,
        grid=(T // tm,),
        in_specs=[
            pl.BlockSpec((tm, D_MODEL), row),
            pl.BlockSpec((1, D_MODEL), const),
            pl.BlockSpec((D_MODEL, IN_COLS), const),
            pl.BlockSpec((tm, LANES), pos),
            pl.BlockSpec((tm, LANES), pos),
        ],
        out_specs=[pl.BlockSpec((tm, w_), row) for w_, _ in outs],
        out_shape=[jax.ShapeDtypeStruct((T, w_), dt) for w_, dt in outs],
        compiler_params=pltpu.CompilerParams(
            dimension_semantics=("arbitrary",), vmem_limit_bytes=VMEM_LIMIT),
        name="inproj",
    )(x2, g, w, cos_t, sin_t)


def _mlstm_kernel(qk_ref, v_ref, o_ref, gate_ref, cw_ref, cb_ref, gb_ref, ng_ref, out_ref,
                  xext_ref, c_ref, m_ref, *, nchunk):
    L = CHUNK
    tb = nchunk * L
    hist = SUBLANES

    @pl.when(pl.program_id(1) == 0)
    def _():
        xext_ref[0:hist, :] = jnp.zeros((hist, 2 * MLSTM_QK_W), F32)
        c_ref[...] = jnp.zeros_like(c_ref)
        m_ref[...] = jnp.zeros_like(m_ref)

    xext_ref[hist:hist + tb, :] = qk_ref[...].astype(F32)
    y = cb_ref[...]
    for j in range(QK_CONV):
        start = hist - (QK_CONV - 1) + j
        y = y + cw_ref[j:j + 1, :] * xext_ref[start:start + tb, :]
    xext_ref[0:hist, :] = xext_ref[tb:tb + hist, :]
    qk = y * _sigmoid(y)

    row_i = lax.broadcasted_iota(jnp.int32, (L, L), 0)
    col_i = lax.broadcasted_iota(jnp.int32, (L, L), 1)
    causal = col_i <= row_i
    tril = causal.astype(BF16)
    triu = (row_i <= col_i).astype(BF16)
    lane = lax.broadcasted_iota(jnp.int32, (L, GATE_W), 1)
    ones_blk = jnp.ones((L, MLSTM_V_DIM), BF16)
    neg_inf = jnp.float32(-jnp.inf)

    for c in range(nchunk):
        rows = slice(c * L, (c + 1) * L)
        q = qk[rows, :MLSTM_QK_W].astype(BF16)
        k = qk[rows, MLSTM_QK_W:] * (MLSTM_QK_DIM ** -0.5)
        k_bf = k.astype(BF16)
        k_t = k.T

        g = gate_ref[rows, :] + gb_ref[...]
        logf = jnp.minimum(g, 0.0) - jnp.log1p(jnp.exp(-jnp.abs(g)))
        comb = jnp.where(lane < MLSTM_HEADS, g, logf)
        comb_t = comb.T[0:SUBLANES, :]
        b_rows = jnp.zeros((SUBLANES, L), F32)
        for part in _split3(comb_t):
            b_rows = b_rows + jnp.dot(part, triu, preferred_element_type=F32)
        b_cols = jnp.zeros((L, GATE_W), F32)
        for part in _split3(logf):
            b_cols = b_cols + jnp.dot(tril, part, preferred_element_type=F32)

        for h in range(MLSTM_HEADS):
            li_row = comb_t[h:h + 1, :]
            b_row = b_rows[MLSTM_HEADS + h:MLSTM_HEADS + h + 1, :]
            b_col = b_cols[:, MLSTM_HEADS + h:MLSTM_HEADS + h + 1]
            r = li_row - b_row
            m = m_ref[h:h + 1, 0:1]

            rmat = jnp.where(causal, jnp.broadcast_to(r, (L, L)), neg_inf)
            a = jnp.maximum(jnp.max(rmat, axis=-1, keepdims=True), m)
            w_intra = jnp.exp(rmat - a)
            w_inter = jnp.exp(m - a)

            qh = q[:, h * MLSTM_QK_DIM:(h + 1) * MLSTM_QK_DIM]
            kh = k_bf[:, h * MLSTM_QK_DIM:(h + 1) * MLSTM_QK_DIM]
            v_ext = jnp.concatenate(
                [v_ref[rows, h * MLSTM_V_DIM:(h + 1) * MLSTM_V_DIM], ones_blk], axis=-1)
            s = lax.dot_general(qh, kh, (((1,), (1,)), ((), ())),
                                preferred_element_type=F32) * w_intra
            c_ext = c_ref[h]
            num_ext = (w_inter * jnp.dot(qh, c_ext.astype(BF16), preferred_element_type=F32)
                       + jnp.dot(s.astype(BF16), v_ext, preferred_element_type=F32))
            den = num_ext[:, MLSTM_V_DIM:]
            nrm = jnp.maximum(jnp.abs(den), jnp.exp(-(b_col + a)))
            hh = num_ext[:, :MLSTM_V_DIM] / nrm
            hn = hh * lax.rsqrt(jnp.mean(hh * hh, axis=-1, keepdims=True) + EPS)
            vs = slice(h * MLSTM_V_DIM, (h + 1) * MLSTM_V_DIM)
            out_ref[rows, vs] = (hn * ng_ref[:, vs]
                                 * _sigmoid(o_ref[rows, vs].astype(F32))).astype(BF16)

            a_last = jnp.maximum(jnp.max(r, axis=-1, keepdims=True), m)
            decay = jnp.exp(m - a_last)
            ws = jnp.exp(r - a_last)
            kw_t = (k_t[h * MLSTM_QK_DIM:(h + 1) * MLSTM_QK_DIM, :] * ws).astype(BF16)
            c_ref[h] = decay * c_ext + jnp.dot(kw_t, v_ext, preferred_element_type=F32)
            m_ref[h:h + 1, :] = jnp.broadcast_to(b_row[:, L - 1:L] + a_last, (1, LANES))


def _mlstm(qk, mv, mo, gates, cw, cb, gb, ng, batch, seq):
    T = qk.shape[0]
    tb = min(TB_MLSTM, seq)
    nb = seq // tb
    row = lambda b, i: (b * nb + i, 0)
    const = lambda b, i: (0, 0)
    return pl.pallas_call(
        functools.partial(_mlstm_kernel, nchunk=tb // CHUNK),
        grid=(batch, nb),
        in_specs=[
            pl.BlockSpec((tb, 2 * MLSTM_QK_W), row),
            pl.BlockSpec((tb, MLSTM_W), row),
            pl.BlockSpec((tb, MLSTM_W), row),
            pl.BlockSpec((tb, GATE_W), row),
            pl.BlockSpec((QK_CONV, 2 * MLSTM_QK_W), const),
            pl.BlockSpec((1, 2 * MLSTM_QK_W), const),
            pl.BlockSpec((1, GATE_W), const),
            pl.BlockSpec((1, MLSTM_W), const),
        ],
        out_specs=pl.BlockSpec((tb, MLSTM_W), row),
        out_shape=jax.ShapeDtypeStruct((T, MLSTM_W), BF16),
        scratch_shapes=[
            pltpu.VMEM((tb + SUBLANES, 2 * MLSTM_QK_W), F32),
            pltpu.VMEM((MLSTM_HEADS, MLSTM_QK_DIM, 2 * MLSTM_V_DIM), F32),
            pltpu.VMEM((SUBLANES, LANES), F32),
        ],
        compiler_params=pltpu.CompilerParams(
            dimension_semantics=("arbitrary", "arbitrary"), vmem_limit_bytes=VMEM_LIMIT),
        name="mlstm",
    )(qk, mv, mo, gates, cw, cb, gb, ng)


def _swa_kernel(sink_ref, q_ref, k_ref, v_ref, kp_ref, vp_ref, out_ref, *, nblk):
    W = WINDOW
    G = ATTN_HEADS // ATTN_KV_HEADS
    half = LANES // 2
    lane = lax.broadcasted_iota(jnp.int32, (W, LANES), 1)
    left = lane < half
    qpos = lax.broadcasted_iota(jnp.int32, (W, 2 * W), 0)
    kpos = lax.broadcasted_iota(jnp.int32, (W, 2 * W), 1)
    band = (kpos > qpos) & (kpos <= qpos + W)
    not_first = pl.program_id(1) > 0
    neg_inf = jnp.float32(-jnp.inf)
    zero = jnp.zeros((), BF16)

    for j in range(nblk):
        rows = slice(j * W, (j + 1) * W)
        if j == 0:
            k2 = jnp.concatenate([kp_ref[...], k_ref[rows, :]], axis=0)
            v2 = jnp.concatenate([vp_ref[...], v_ref[rows, :]], axis=0)
            valid = band & ((kpos >= W) | not_first)
        else:
            k2 = k_ref[(j - 1) * W:(j + 1) * W, :]
            v2 = v_ref[(j - 1) * W:(j + 1) * W, :]
            valid = band
        qb = q_ref[rows, :]
        qs = jnp.concatenate(
            [jnp.where(left, qb[:, c * LANES:(c + 1) * LANES], zero) for c in range(G)]
            + [jnp.where(left, zero, qb[:, c * LANES:(c + 1) * LANES]) for c in range(G)], axis=0)
        s_all = lax.dot_general(qs, k2, (((1,), (1,)), ((), ())),
                                preferred_element_type=F32)
        ps, invs = [], []
        for h in range(ATTN_HEADS):
            sink = sink_ref[h]
            s = jnp.where(valid, s_all[h * W:(h + 1) * W, :], neg_inf)
            mx = jnp.maximum(jnp.max(s, axis=-1, keepdims=True), sink)
            p = jnp.exp(s - mx)
            denom = jnp.sum(p, axis=-1, keepdims=True) + jnp.exp(sink - mx)
            ps.append(p.astype(BF16))
            invs.append(1.0 / denom)
        pv = jnp.dot(jnp.concatenate(ps, axis=0), v2, preferred_element_type=F32)
        for c in range(G):
            lo = pv[c * W:(c + 1) * W, :] * invs[c]
            hi = pv[(G + c) * W:(G + c + 1) * W, :] * invs[G + c]
            out_ref[rows, c * LANES:(c + 1) * LANES] = jnp.where(left, lo, hi).astype(BF16)


def _swa(sinks, aq, ak, av, batch, seq):
    T = aq.shape[0]
    tq = min(TQ_SWA, seq)
    nb = seq // tq
    nblk = tq // WINDOW
    nwin = seq // WINDOW
    row = lambda b, i: (b * nb + i, 0)
    prev = lambda b, i: (b * nwin + jnp.maximum(i * nblk - 1, 0), 0)
    return pl.pallas_call(
        functools.partial(_swa_kernel, nblk=nblk),
        grid=(batch, nb),
        in_specs=[
            pl.BlockSpec(memory_space=pltpu.SMEM),
            pl.BlockSpec((tq, ATTN_W), row),
            pl.BlockSpec((tq, ATTN_KV_W), row),
            pl.BlockSpec((tq, ATTN_KV_W), row),
            pl.BlockSpec((WINDOW, ATTN_KV_W), prev),
            pl.BlockSpec((WINDOW, ATTN_KV_W), prev),
        ],
        out_specs=pl.BlockSpec((tq, ATTN_W), row),
        out_shape=jax.ShapeDtypeStruct((T, ATTN_W), BF16),
        compiler_params=pltpu.CompilerParams(
            dimension_semantics=("arbitrary", "arbitrary"), vmem_limit_bytes=VMEM_LIMIT),
        name="swa",
    )(sinks, aq, ak, av, ak, av)


def _outproj_kernel(m_ref, a_ref, wm_ref, wa_ref, x_ref, gpost_ref, gpre_ref, x1_ref, h2_ref):
    y = (jnp.dot(m_ref[...], wm_ref[...], preferred_element_type=F32)
         + jnp.dot(a_ref[...], wa_ref[...], preferred_element_type=F32))
    x1 = x_ref[...] + _rms(y, gpost_ref[...])
    x1_ref[...] = x1
    h2_ref[...] = _rms(x1, gpre_ref[...]).astype(BF16)


def _outproj(m_out, a_out, wm, wa, x2, gpost, gpre):
    T = x2.shape[0]
    tm = min(TM_PROJ, T)
    row = lambda i: (i, 0)
    const = lambda i: (0, 0)
    return pl.pallas_call(
        _outproj_kernel,
        grid=(T // tm,),
        in_specs=[
            pl.BlockSpec((tm, MLSTM_W), row),
            pl.BlockSpec((tm, ATTN_W), row),
            pl.BlockSpec((MLSTM_W, D_MODEL), const),
            pl.BlockSpec((ATTN_W, D_MODEL), const),
            pl.BlockSpec((tm, D_MODEL), row),
            pl.BlockSpec((1, D_MODEL), const),
            pl.BlockSpec((1, D_MODEL), const),
        ],
        out_specs=[pl.BlockSpec((tm, D_MODEL), row), pl.BlockSpec((tm, D_MODEL), row)],
        out_shape=[jax.ShapeDtypeStruct((T, D_MODEL), F32), jax.ShapeDtypeStruct((T, D_MODEL), BF16)],
        compiler_params=pltpu.CompilerParams(
            dimension_semantics=("arbitrary",), vmem_limit_bytes=VMEM_LIMIT),
        name="outproj",
    )(m_out, a_out, wm, wa, x2, gpost, gpre)


def _ffn_kernel(h_ref, x_ref, wg_ref, wv_ref, wd_ref, cg_ref, cv_ref, gpost_ref, out_ref,
                acc_ref, carry_g_ref, carry_v_ref, *, nf, tiles_per_seq):
    tm = h_ref.shape[0]
    tf = wg_ref.shape[2]
    h = h_ref[...]
    seq_start = (pl.program_id(0) % tiles_per_seq) == 0
    row8 = lax.broadcasted_iota(jnp.int32, (SUBLANES, tf), 0)

    def conv(u, prev, cw):
        s1 = pltpu.roll(u, 1, axis=0)
        s2 = pltpu.roll(u, 2, axis=0)
        f1 = jnp.where(row8 < 1, pltpu.roll(prev, 1, axis=0), s1[0:SUBLANES, :])
        f2 = jnp.where(row8 < 2, pltpu.roll(prev, 2, axis=0), s2[0:SUBLANES, :])
        s1 = jnp.concatenate([f1, s1[SUBLANES:, :]], axis=0)
        s2 = jnp.concatenate([f2, s2[SUBLANES:, :]], axis=0)
        return cw[2:3, :] * u + cw[1:2, :] * s1 + cw[0:1, :] * s2

    def body(f, carry):
        ug = jnp.dot(h, wg_ref[f], preferred_element_type=F32)
        uv = jnp.dot(h, wv_ref[f], preferred_element_type=F32)
        pg = jnp.where(seq_start, 0.0, carry_g_ref[f])
        pv = jnp.where(seq_start, 0.0, carry_v_ref[f])
        carry_g_ref[f] = ug[tm - SUBLANES:, :]
        carry_v_ref[f] = uv[tm - SUBLANES:, :]
        gate = conv(ug, pg, cg_ref[f])
        val = conv(uv, pv, cv_ref[f])
        act = (gate * _sigmoid(gate) * val).astype(BF16)
        contrib = jnp.dot(act, wd_ref[f], preferred_element_type=F32)

        @pl.when(f == 0)
        def _():
            acc_ref[...] = contrib

        @pl.when(f > 0)
        def _():
            acc_ref[...] += contrib
        return carry

    lax.fori_loop(0, nf, body, 0)
    out_ref[...] = x_ref[...] + _rms(acc_ref[...], gpost_ref[...])


def _ffn(h2, x1, wg, wv, wd, cg, cv, gpost, seq):
    T = h2.shape[0]
    tm = min(TM_FFN, seq)
    nf, _, tf = wg.shape
    row = lambda i: (i, 0)
    const2 = lambda i: (0, 0)
    const3 = lambda i: (0, 0, 0)
    return pl.pallas_call(
        functools.partial(_ffn_kernel, nf=nf, tiles_per_seq=seq // tm),
        grid=(T // tm,),
        in_specs=[
            pl.BlockSpec((tm, D_MODEL), row),
            pl.BlockSpec((tm, D_MODEL), row),
            pl.BlockSpec((nf, D_MODEL, tf), const3),
            pl.BlockSpec((nf, D_MODEL, tf), const3),
            pl.BlockSpec((nf, tf, D_MODEL), const3),
            pl.BlockSpec((nf, SUBLANES, tf), const3),
            pl.BlockSpec((nf, SUBLANES, tf), const3),
            pl.BlockSpec((1, D_MODEL), const2),
        ],
        out_specs=pl.BlockSpec((tm, D_MODEL), row),
        out_shape=jax.ShapeDtypeStruct((T, D_MODEL), F32),
        scratch_shapes=[
            pltpu.VMEM((tm, D_MODEL), F32),
            pltpu.VMEM((nf, SUBLANES, tf), F32),
            pltpu.VMEM((nf, SUBLANES, tf), F32),
        ],
        compiler_params=pltpu.CompilerParams(
            dimension_semantics=("arbitrary",), vmem_limit_bytes=VMEM_LIMIT),
        name="ffn",
    )(h2, x1, wg, wv, wd, cg, cv, gpost)


def _pair_order():
    G = ATTN_HEADS // ATTN_KV_HEADS
    idx = []
    for c in range(G):
        idx += list(range(c * ATTN_HEAD_DIM, (c + 1) * ATTN_HEAD_DIM))
        idx += list(range((c + G) * ATTN_HEAD_DIM, (c + G + 1) * ATTN_HEAD_DIM))
    return np.asarray(idx, np.int32)


def _rope_tables(seq):
    half = ATTN_HEAD_DIM // 2
    inv = 1.0 / (ROPE_THETA ** (jnp.arange(0, ATTN_HEAD_DIM, 2, dtype=F32) / ATTN_HEAD_DIM))
    ang = jnp.arange(seq, dtype=F32)[:, None] * inv[None, :]
    cos, sin = jnp.cos(ang), jnp.sin(ang)
    reps = LANES // ATTN_HEAD_DIM
    cos_t = jnp.concatenate([cos, cos] * reps, axis=-1)
    sin_t = jnp.concatenate([-sin, sin] * reps, axis=-1)
    assert cos_t.shape == (seq, LANES) and half * 2 * reps == LANES
    return cos_t, sin_t


def _pad_rows(a, rows):
    return jnp.concatenate([a, jnp.zeros((rows - a.shape[0],) + a.shape[1:], a.dtype)], axis=0)


def kernel(x, g_pre_mix, w_in, qk_conv_w, qk_conv_b, gate_bias, mh_norm_g, attn_sinks, w_out,
           g_post_mix, g_pre_ffn, w_up, ffn_conv_w, w_down, g_post_ffn):
    batch, seq, _ = x.shape
    depth = w_in.shape[0]
    T = batch * seq
    perm = _pair_order()
    cos_t, sin_t = _rope_tables(seq)
    nf = D_FF // TF_FFN

    n_gate = 2 * MLSTM_HEADS
    o_q = 0
    o_v = 2 * MLSTM_QK_W
    o_o = o_v + MLSTM_W
    o_g = o_o + MLSTM_W
    o_aq = o_g + n_gate
    o_ak = o_aq + ATTN_W
    o_av = o_ak + ATTN_KV_W

    x2 = x.reshape(T, D_MODEL)
    for l in range(depth):
        wl = w_in[l]
        w_a = jnp.concatenate([
            wl[:, o_q:o_g],
            wl[:, o_aq:o_ak][:, perm],
            wl[:, o_ak:o_av + ATTN_KV_W],
            wl[:, o_g:o_aq],
            jnp.zeros((D_MODEL, GATE_W - n_gate), wl.dtype),
        ], axis=1).astype(BF16)
        gb = jnp.concatenate([gate_bias[l], jnp.zeros((GATE_W - n_gate,), F32)])[None, :]
        wm = w_out[l, :MLSTM_W].astype(BF16)
        wa = w_out[l, MLSTM_W:][perm].astype(BF16)
        wg = w_up[l, :, :D_FF].reshape(D_MODEL, nf, TF_FFN).transpose(1, 0, 2).astype(BF16)
        wv = w_up[l, :, D_FF:].reshape(D_MODEL, nf, TF_FFN).transpose(1, 0, 2).astype(BF16)
        wd = w_down[l].reshape(nf, TF_FFN, D_MODEL).astype(BF16)
        cw = _pad_rows(ffn_conv_w[l], SUBLANES)
        cg = cw[:, :D_FF].reshape(SUBLANES, nf, TF_FFN).transpose(1, 0, 2)
        cv = cw[:, D_FF:].reshape(SUBLANES, nf, TF_FFN).transpose(1, 0, 2)

        qk, mv, mo, gates, aq, ak, av = _inproj(x2, g_pre_mix[l][None, :], w_a, cos_t, sin_t, seq)
        m_out = _mlstm(qk, mv, mo, gates, qk_conv_w[l], qk_conv_b[l][None, :], gb,
                       mh_norm_g[l][None, :], batch, seq)
        a_out = _swa(attn_sinks[l], aq, ak, av, batch, seq)
        x1, h2 = _outproj(m_out, a_out, wm, wa, x2, g_post_mix[l][None, :], g_pre_ffn[l][None, :])
        x2 = _ffn(h2, x1, wg, wv, wd, cg, cv, g_post_ffn[l][None, :], seq)
    return x2.reshape(batch, seq, D_MODEL)
```

```python
import functools

import jax
import jax.numpy as jnp
import numpy as np
from jax import lax
from jax.experimental import pallas as pl
from jax.experimental.pallas import tpu as pltpu

F32 = jnp.float32
BF16 = jnp.bfloat16

D_MODEL = 1024
MLSTM_HEADS = 4
MLSTM_QK_DIM = 64
MLSTM_V_DIM = 128
MLSTM_QK_W = MLSTM_HEADS * MLSTM_QK_DIM
MLSTM_W = MLSTM_HEADS * MLSTM_V_DIM
CHUNK = 128
QK_CONV = 4
ATTN_HEADS = 8
ATTN_KV_HEADS = 2
ATTN_HEAD_DIM = 64
ATTN_W = ATTN_HEADS * ATTN_HEAD_DIM
ATTN_KV_W = ATTN_KV_HEADS * ATTN_HEAD_DIM
WINDOW = 128
ROPE_THETA = 10000.0
D_FF = 2816
FFN_CONV = 3
EPS = 1e-6

LANES = 128
SUBLANES = 8
GATE_W = LANES
IN_COLS = 2 * MLSTM_QK_W + 2 * MLSTM_W + ATTN_W + 2 * ATTN_KV_W + GATE_W

TM_PROJ = 512
TB_MLSTM = 512
TQ_SWA = 512
TM_FFN = 1024
TF_FFN = 256
VMEM_LIMIT = 56 * 1024 * 1024


def _sigmoid(x):
    return 1.0 / (1.0 + jnp.exp(-x))


def _rms(x, g):
    return x * lax.rsqrt(jnp.mean(x * x, axis=-1, keepdims=True) + EPS) * g


def _split3(x):
    hi = x.astype(BF16)
    r1 = x - hi.astype(F32)
    mid = r1.astype(BF16)
    lo = (r1 - mid.astype(F32)).astype(BF16)
    return hi, mid, lo


def _inproj_kernel(x_ref, g_ref, w_ref, cos_ref, sin_ref,
                   qk_ref, mv_ref, mo_ref, gate_ref, aq_ref, ak_ref, av_ref):
    h = _rms(x_ref[...], g_ref[...]).astype(BF16)

    def proj(lo, width):
        return jnp.dot(h, w_ref[:, lo:lo + width], preferred_element_type=F32)

    off = 0
    qk_ref[...] = proj(off, 2 * MLSTM_QK_W).astype(BF16)
    off += 2 * MLSTM_QK_W
    mv_ref[...] = proj(off, MLSTM_W).astype(BF16)
    off += MLSTM_W
    mo_ref[...] = proj(off, MLSTM_W).astype(BF16)
    off += MLSTM_W

    cos = cos_ref[...]
    sin = sin_ref[...]
    lane = lax.broadcasted_iota(jnp.int32, cos.shape, 1)
    first_half = (lane % ATTN_HEAD_DIM) < (ATTN_HEAD_DIM // 2)

    def rope(t):
        partner = jnp.where(first_half,
                            pltpu.roll(t, LANES - ATTN_HEAD_DIM // 2, axis=1),
                            pltpu.roll(t, ATTN_HEAD_DIM // 2, axis=1))
        return t * cos + partner * sin

    aq = proj(off, ATTN_W)
    off += ATTN_W
    scale = ATTN_HEAD_DIM ** -0.5
    for c in range(ATTN_W // LANES):
        sl = slice(c * LANES, (c + 1) * LANES)
        aq_ref[:, sl] = (rope(aq[:, sl]) * scale).astype(BF16)
    ak_ref[...] = rope(proj(off, ATTN_KV_W)).astype(BF16)
    off += ATTN_KV_W
    av_ref[...] = proj(off, ATTN_KV_W).astype(BF16)
    off += ATTN_KV_W
    gate_ref[...] = proj(off, GATE_W)


def _inproj(x2, g, w, cos_t, sin_t, seq):
    T = x2.shape[0]
    tm = min(TM_PROJ, seq)
    nseq = seq // tm
    row = lambda i: (i, 0)
    const = lambda i: (0, 0)
    pos = lambda i: (i % nseq, 0)
    outs = [
        (2 * MLSTM_QK_W, BF16), (MLSTM_W, BF16), (MLSTM_W, BF16), (GATE_W, F32),
        (ATTN_W, BF16), (ATTN_KV_W, BF16), (ATTN_KV_W, BF16),
    ]
    return pl.pallas_call(
        _inproj_kernel,
        grid=(T // tm,),
        in_specs=[
            pl.BlockSpec((tm, D_MODEL), row),
            pl.BlockSpec((1, D_MODEL), const),
            pl.BlockSpec((D_MODEL, IN_COLS), const),
            pl.BlockSpec((tm, LANES), pos),
            pl.BlockSpec((tm, LANES), pos),
        ],
        out_specs=[pl.BlockSpec((tm, w_), row) for w_, _ in outs],
        out_shape=[jax.ShapeDtypeStruct((T, w_), dt) for w_, dt in outs],
        compiler_params=pltpu.CompilerParams(
            dimension_semantics=("arbitrary",), vmem_limit_bytes=VMEM_LIMIT),
        name="inproj",
    )(x2, g, w, cos_t, sin_t)


def _mlstm_kernel(qk_ref, v_ref, o_ref, gate_ref, cw_ref, cb_ref, gb_ref, ng_ref, out_ref,
                  xext_ref, c_ref, m_ref, *, nchunk):
    L = CHUNK
    tb = nchunk * L
    hist = SUBLANES

    @pl.when(pl.program_id(1) == 0)
    def _():
        xext_ref[0:hist, :] = jnp.zeros((hist, 2 * MLSTM_QK_W), F32)
        c_ref[...] = jnp.zeros_like(c_ref)
        m_ref[...] = jnp.zeros_like(m_ref)

    xext_ref[hist:hist + tb, :] = qk_ref[...].astype(F32)
    y = cb_ref[...]
    for j in range(QK_CONV):
        start = hist - (QK_CONV - 1) + j
        y = y + cw_ref[j:j + 1, :] * xext_ref[start:start + tb, :]
    xext_ref[0:hist, :] = xext_ref[tb:tb + hist, :]
    qk = y * _sigmoid(y)

    row_i = lax.broadcasted_iota(jnp.int32, (L, L), 0)
    col_i = lax.broadcasted_iota(jnp.int32, (L, L), 1)
    causal = col_i <= row_i
    tril = causal.astype(BF16)
    triu = (row_i <= col_i).astype(BF16)
    lane = lax.broadcasted_iota(jnp.int32, (L, GATE_W), 1)
    ones_blk = jnp.ones((L, MLSTM_V_DIM), BF16)
    neg_inf = jnp.float32(-jnp.inf)

    for c in range(nchunk):
        rows = slice(c * L, (c + 1) * L)
        q = qk[rows, :MLSTM_QK_W].astype(BF16)
        k = qk[rows, MLSTM_QK_W:] * (MLSTM_QK_DIM ** -0.5)
        k_bf = k.astype(BF16)
        k_t = k.T

        g = gate_ref[rows, :] + gb_ref[...]
        logf = jnp.minimum(g, 0.0) - jnp.log1p(jnp.exp(-jnp.abs(g)))
        comb = jnp.where(lane < MLSTM_HEADS, g, logf)
        comb_t = comb.T[0:SUBLANES, :]
        b_rows = jnp.zeros((SUBLANES, L), F32)
        for part in _split3(comb_t):
            b_rows = b_rows + jnp.dot(part, triu, preferred_element_type=F32)
        b_cols = jnp.zeros((L, GATE_W), F32)
        for part in _split3(logf):
            b_cols = b_cols + jnp.dot(tril, part, preferred_element_type=F32)

        for h in range(MLSTM_HEADS):
            li_row = comb_t[h:h + 1, :]
            b_row = b_rows[MLSTM_HEADS + h:MLSTM_HEADS + h + 1, :]
            b_col = b_cols[:, MLSTM_HEADS + h:MLSTM_HEADS + h + 1]
            r = li_row - b_row
            m = m_ref[h:h + 1, 0:1]

            rmat = jnp.where(causal, jnp.broadcast_to(r, (L, L)), neg_inf)
            a = jnp.maximum(jnp.max(rmat, axis=-1, keepdims=True), m)
            w_intra = jnp.exp(rmat - a)
            w_inter = jnp.exp(m - a)

            qh = q[:, h * MLSTM_QK_DIM:(h + 1) * MLSTM_QK_DIM]
            kh = k_bf[:, h * MLSTM_QK_DIM:(h + 1) * MLSTM_QK_DIM]
            v_ext = jnp.concatenate(
                [v_ref[rows, h * MLSTM_V_DIM:(h + 1) * MLSTM_V_DIM], ones_blk], axis=-1)
            s = lax.dot_general(qh, kh, (((1,), (1,)), ((), ())),
                                preferred_element_type=F32) * w_intra
            c_ext = c_ref[h]
            num_ext = (w_inter * jnp.dot(qh, c_ext.astype(BF16), preferred_element_type=F32)
                       + jnp.dot(s.astype(BF16), v_ext, preferred_element_type=F32))
            den = num_ext[:, MLSTM_V_DIM:]
            nrm = jnp.maximum(jnp.abs(den), jnp.exp(-(b_col + a)))
            hh = num_ext[:, :MLSTM_V_DIM] / nrm
            hn = hh * lax.rsqrt(jnp.mean(hh * hh, axis=-1, keepdims=True) + EPS)
            vs = slice(h * MLSTM_V_DIM, (h + 1) * MLSTM_V_DIM)
            out_ref[rows, vs] = (hn * ng_ref[:, vs]
                                 * _sigmoid(o_ref[rows, vs].astype(F32))).astype(BF16)

            a_last = jnp.maximum(jnp.max(r, axis=-1, keepdims=True), m)
            decay = jnp.exp(m - a_last)
            ws = jnp.exp(r - a_last)
            kw_t = (k_t[h * MLSTM_QK_DIM:(h + 1) * MLSTM_QK_DIM, :] * ws).astype(BF16)
            c_ref[h] = decay * c_ext + jnp.dot(kw_t, v_ext, preferred_element_type=F32)
            m_ref[h:h + 1, :] = jnp.broadcast_to(b_row[:, L - 1:L] + a_last, (1, LANES))


def _mlstm(qk, mv, mo, gates, cw, cb, gb, ng, batch, seq):
    T = qk.shape[0]
    tb = min(TB_MLSTM, seq)
    nb = seq // tb
    row = lambda b, i: (b * nb + i, 0)
    const = lambda b, i: (0, 0)
    return pl.pallas_call(
        functools.partial(_mlstm_kernel, nchunk=tb // CHUNK),
        grid=(batch, nb),
        in_specs=[
            pl.BlockSpec((tb, 2 * MLSTM_QK_W), row),
            pl.BlockSpec((tb, MLSTM_W), row),
            pl.BlockSpec((tb, MLSTM_W), row),
            pl.BlockSpec((tb, GATE_W), row),
            pl.BlockSpec((QK_CONV, 2 * MLSTM_QK_W), const),
            pl.BlockSpec((1, 2 * MLSTM_QK_W), const),
            pl.BlockSpec((1, GATE_W), const),
            pl.BlockSpec((1, MLSTM_W), const),
        ],
        out_specs=pl.BlockSpec((tb, MLSTM_W), row),
        out_shape=jax.ShapeDtypeStruct((T, MLSTM_W), BF16),
        scratch_shapes=[
            pltpu.VMEM((tb + SUBLANES, 2 * MLSTM_QK_W), F32),
            pltpu.VMEM((MLSTM_HEADS, MLSTM_QK_DIM, 2 * MLSTM_V_DIM), F32),
            pltpu.VMEM((SUBLANES, LANES), F32),
        ],
        compiler_params=pltpu.CompilerParams(
            dimension_semantics=("arbitrary", "arbitrary"), vmem_limit_bytes=VMEM_LIMIT),
        name="mlstm",
    )(qk, mv, mo, gates, cw, cb, gb, ng)


def _swa_kernel(sink_ref, q_ref, k_ref, v_ref, kp_ref, vp_ref, out_ref, *, nblk):
    W = WINDOW
    G = ATTN_HEADS // ATTN_KV_HEADS
    half = LANES // 2
    lane = lax.broadcasted_iota(jnp.int32, (W, LANES), 1)
    left = lane < half
    qpos = lax.broadcasted_iota(jnp.int32, (W, 2 * W), 0)
    kpos = lax.broadcasted_iota(jnp.int32, (W, 2 * W), 1)
    band = (kpos > qpos) & (kpos <= qpos + W)
    not_first = pl.program_id(1) > 0
    neg_inf = jnp.float32(-jnp.inf)
    zero = jnp.zeros((), BF16)

    for j in range(nblk):
        rows = slice(j * W, (j + 1) * W)
        if j == 0:
            k2 = jnp.concatenate([kp_ref[...], k_ref[rows, :]], axis=0)
            v2 = jnp.concatenate([vp_ref[...], v_ref[rows, :]], axis=0)
            valid = band & ((kpos >= W) | not_first)
        else:
            k2 = k_ref[(j - 1) * W:(j + 1) * W, :]
            v2 = v_ref[(j - 1) * W:(j + 1) * W, :]
            valid = band
        qb = q_ref[rows, :]
        qs = jnp.concatenate(
            [jnp.where(left, qb[:, c * LANES:(c + 1) * LANES], zero) for c in range(G)]
            + [jnp.where(left, zero, qb[:, c * LANES:(c + 1) * LANES]) for c in range(G)], axis=0)
        s_all = lax.dot_general(qs, k2, (((1,), (1,)), ((), ())),
                                preferred_element_type=F32)
        ps, invs = [], []
        for h in range(ATTN_HEADS):
            sink = sink_ref[h]
            s = jnp.where(valid, s_all[h * W:(h + 1) * W, :], neg_inf)
            mx = jnp.maximum(jnp.max(s, axis=-1, keepdims=True), sink)
            p = jnp.exp(s - mx)
            denom = jnp.sum(p, axis=-1, keepdims=True) + jnp.exp(sink - mx)
            ps.append(p.astype(BF16))
            invs.append(1.0 / denom)
        pv = jnp.dot(jnp.concatenate(ps, axis=0), v2, preferred_element_type=F32)
        for c in range(G):
            lo = pv[c * W:(c + 1) * W, :] * invs[c]
            hi = pv[(G + c) * W:(G + c + 1) * W, :] * invs[G + c]
            out_ref[rows, c * LANES:(c + 1) * LANES] = jnp.where(left, lo, hi).astype(BF16)


def _swa(sinks, aq, ak, av, batch, seq):
    T = aq.shape[0]
    tq = min(TQ_SWA, seq)
    nb = seq // tq
    nblk = tq // WINDOW
    nwin = seq // WINDOW
    row = lambda b, i: (b * nb + i, 0)
    prev = lambda b, i: (b * nwin + jnp.maximum(i * nblk - 1, 0), 0)
    return pl.pallas_call(
        functools.partial(_swa_kernel, nblk=nblk),
        grid=(batch, nb),
        in_specs=[
            pl.BlockSpec(memory_space=pltpu.SMEM),
            pl.BlockSpec((tq, ATTN_W), row),
            pl.BlockSpec((tq, ATTN_KV_W), row),
            pl.BlockSpec((tq, ATTN_KV_W), row),
            pl.BlockSpec((WINDOW, ATTN_KV_W), prev),
            pl.BlockSpec((WINDOW, ATTN_KV_W), prev),
        ],
        out_specs=pl.BlockSpec((tq, ATTN_W), row),
        out_shape=jax.ShapeDtypeStruct((T, ATTN_W), BF16),
        compiler_params=pltpu.CompilerParams(
            dimension_semantics=("arbitrary", "arbitrary"), vmem_limit_bytes=VMEM_LIMIT),
        name="swa",
    )(sinks, aq, ak, av, ak, av)


def _outproj_kernel(m_ref, a_ref, wm_ref, wa_ref, x_ref, gpost_ref, gpre_ref, x1_ref, h2_ref):
    y = (jnp.dot(m_ref[...], wm_ref[...], preferred_element_type=F32)
         + jnp.dot(a_ref[...], wa_ref[...], preferred_element_type=F32))
    x1 = x_ref[...] + _rms(y, gpost_ref[...])
    x1_ref[...] = x1
    h2_ref[...] = _rms(x1, gpre_ref[...]).astype(BF16)


def _outproj(m_out, a_out, wm, wa, x2, gpost, gpre):
    T = x2.shape[0]
    tm = min(TM_PROJ, T)
    row = lambda i: (i, 0)
    const = lambda i: (0, 0)
    return pl.pallas_call(
        _outproj_kernel,
        grid=(T // tm,),
        in_specs=[
            pl.BlockSpec((tm, MLSTM_W), row),
            pl.BlockSpec((tm, ATTN_W), row),
            pl.BlockSpec((MLSTM_W, D_MODEL), const),
            pl.BlockSpec((ATTN_W, D_MODEL), const),
            pl.BlockSpec((tm, D_MODEL), row),
            pl.BlockSpec((1, D_MODEL), const),
            pl.BlockSpec((1, D_MODEL), const),
        ],
        out_specs=[pl.BlockSpec((tm, D_MODEL), row), pl.BlockSpec((tm, D_MODEL), row)],
        out_shape=[jax.ShapeDtypeStruct((T, D_MODEL), F32), jax.ShapeDtypeStruct((T, D_MODEL), BF16)],
        compiler_params=pltpu.CompilerParams(
            dimension_semantics=("arbitrary",), vmem_limit_bytes=VMEM_LIMIT),
        name="outproj",
    )(m_out, a_out, wm, wa, x2, gpost, gpre)


def _ffn_kernel(h_ref, x_ref, wg_ref, wv_ref, wd_ref, cg_ref, cv_ref, gpost_ref, out_ref,
                hs_ref, act_ref, ua_ref, ub_ref, carry_g_ref, carry_v_ref, *, nf, tiles_per_seq):
    tm = h_ref.shape[0]
    tf = wg_ref.shape[2]
    hs_ref[...] = h_ref[...]
    seq_start = (pl.program_id(0) % tiles_per_seq) == 0
    sub = lax.broadcasted_iota(jnp.int32, (1, SUBLANES, tf), 1)

    def conv(u, prev, cw):
        u3 = jnp.concatenate([prev[None], u.reshape(tm // SUBLANES, SUBLANES, tf)], axis=0)
        cur, prv = u3[1:], u3[:-1]
        s1 = pltpu.roll(jnp.where(sub >= SUBLANES - 1, prv, cur), 1, axis=1)
        s2 = pltpu.roll(jnp.where(sub >= SUBLANES - 2, prv, cur), 2, axis=1)
        y = cw[2:3, :][None] * cur + cw[1:2, :][None] * s1 + cw[0:1, :][None] * s2
        return y.reshape(tm, tf)

    def up(f, u_ref):
        h = hs_ref[...]
        u_ref[:, :tf] = jnp.dot(h, wg_ref[f], preferred_element_type=F32)
        u_ref[:, tf:] = jnp.dot(h, wv_ref[f], preferred_element_type=F32)

    def gate_act(f, u_ref):
        ug = u_ref[:, :tf]
        uv = u_ref[:, tf:]
        pg = jnp.where(seq_start, 0.0, carry_g_ref[f])
        pv = jnp.where(seq_start, 0.0, carry_v_ref[f])
        carry_g_ref[f] = ug[tm - SUBLANES:, :]
        carry_v_ref[f] = uv[tm - SUBLANES:, :]
        gate = conv(ug, pg, cg_ref[f])
        val = conv(uv, pv, cv_ref[f])
        col = pl.multiple_of(f * tf, tf)
        act_ref[:, pl.ds(col, tf)] = (gate * _sigmoid(gate) * val).astype(BF16)

    up(0, ua_ref)

    def body(i, carry):
        f = 2 * i
        up(f + 1, ub_ref)
        gate_act(f, ua_ref)
        up(f + 2, ua_ref)
        gate_act(f + 1, ub_ref)
        return carry

    lax.fori_loop(0, (nf - 1) // 2, body, 0)
    gate_act(nf - 1, ua_ref)

    y = jnp.dot(act_ref[...], wd_ref[...], preferred_element_type=F32)
    out_ref[...] = x_ref[...] + _rms(y, gpost_ref[...])


def _ffn(h2, x1, wg, wv, wd, cg, cv, gpost, seq):
    T = h2.shape[0]
    tm = min(TM_FFN, seq)
    nf, _, tf = wg.shape
    assert nf % 2 == 1
    row = lambda i: (i, 0)
    const2 = lambda i: (0, 0)
    const3 = lambda i: (0, 0, 0)
    return pl.pallas_call(
        functools.partial(_ffn_kernel, nf=nf, tiles_per_seq=seq // tm),
        grid=(T // tm,),
        in_specs=[
            pl.BlockSpec((tm, D_MODEL), row),
            pl.BlockSpec((tm, D_MODEL), row),
            pl.BlockSpec((nf, D_MODEL, tf), const3),
            pl.BlockSpec((nf, D_MODEL, tf), const3),
            pl.BlockSpec((nf * tf, D_MODEL), const2),
            pl.BlockSpec((nf, SUBLANES, tf), const3),
            pl.BlockSpec((nf, SUBLANES, tf), const3),
            pl.BlockSpec((1, D_MODEL), const2),
        ],
        out_specs=pl.BlockSpec((tm, D_MODEL), row),
        out_shape=jax.ShapeDtypeStruct((T, D_MODEL), F32),
        scratch_shapes=[
            pltpu.VMEM((tm, D_MODEL), BF16),
            pltpu.VMEM((tm, nf * tf), BF16),
            pltpu.VMEM((tm, 2 * tf), F32),
            pltpu.VMEM((tm, 2 * tf), F32),
            pltpu.VMEM((nf, SUBLANES, tf), F32),
            pltpu.VMEM((nf, SUBLANES, tf), F32),
        ],
        compiler_params=pltpu.CompilerParams(
            dimension_semantics=("arbitrary",), vmem_limit_bytes=VMEM_LIMIT),
        name="ffn",
    )(h2, x1, wg, wv, wd, cg, cv, gpost)


def _pair_order():
    G = ATTN_HEADS // ATTN_KV_HEADS
    idx = []
    for c in range(G):
        idx += list(range(c * ATTN_HEAD_DIM, (c + 1) * ATTN_HEAD_DIM))
        idx += list(range((c + G) * ATTN_HEAD_DIM, (c + G + 1) * ATTN_HEAD_DIM))
    return np.asarray(idx, np.int32)


def _rope_tables(seq):
    half = ATTN_HEAD_DIM // 2
    inv = 1.0 / (ROPE_THETA ** (jnp.arange(0, ATTN_HEAD_DIM, 2, dtype=F32) / ATTN_HEAD_DIM))
    ang = jnp.arange(seq, dtype=F32)[:, None] * inv[None, :]
    cos, sin = jnp.cos(ang), jnp.sin(ang)
    reps = LANES // ATTN_HEAD_DIM
    cos_t = jnp.concatenate([cos, cos] * reps, axis=-1)
    sin_t = jnp.concatenate([-sin, sin] * reps, axis=-1)
    assert cos_t.shape == (seq, LANES) and half * 2 * reps == LANES
    return cos_t, sin_t


def _pad_rows(a, rows):
    return jnp.concatenate([a, jnp.zeros((rows - a.shape[0],) + a.shape[1:], a.dtype)], axis=0)


def kernel(x, g_pre_mix, w_in, qk_conv_w, qk_conv_b, gate_bias, mh_norm_g, attn_sinks, w_out,
           g_post_mix, g_pre_ffn, w_up, ffn_conv_w, w_down, g_post_ffn):
    batch, seq, _ = x.shape
    depth = w_in.shape[0]
    T = batch * seq
    perm = _pair_order()
    cos_t, sin_t = _rope_tables(seq)
    nf = D_FF // TF_FFN

    n_gate = 2 * MLSTM_HEADS
    o_q = 0
    o_v = 2 * MLSTM_QK_W
    o_o = o_v + MLSTM_W
    o_g = o_o + MLSTM_W
    o_aq = o_g + n_gate
    o_ak = o_aq + ATTN_W
    o_av = o_ak + ATTN_KV_W

    x2 = x.reshape(T, D_MODEL)
    for l in range(depth):
        wl = w_in[l]
        w_a = jnp.concatenate([
            wl[:, o_q:o_g],
            wl[:, o_aq:o_ak][:, perm],
            wl[:, o_ak:o_av + ATTN_KV_W],
            wl[:, o_g:o_aq],
            jnp.zeros((D_MODEL, GATE_W - n_gate), wl.dtype),
        ], axis=1).astype(BF16)
        gb = jnp.concatenate([gate_bias[l], jnp.zeros((GATE_W - n_gate,), F32)])[None, :]
        wm = w_out[l, :MLSTM_W].astype(BF16)
        wa = w_out[l, MLSTM_W:][perm].astype(BF16)
        wg = w_up[l, :, :D_FF].reshape(D_MODEL, nf, TF_FFN).transpose(1, 0, 2).astype(BF16)
        wv = w_up[l, :, D_FF:].reshape(D_MODEL, nf, TF_FFN).transpose(1, 0, 2).astype(BF16)
        wd = w_down[l].astype(BF16)
        cw = _pad_rows(ffn_conv_w[l], SUBLANES)
        cg = cw[:, :D_FF].reshape(SUBLANES, nf, TF_FFN).transpose(1, 0, 2)
        cv = cw[:, D_FF:].reshape(SUBLANES, nf, TF_FFN).transpose(1, 0, 2)

        qk, mv, mo, gates, aq, ak, av = _inproj(x2, g_pre_mix[l][None, :], w_a, cos_t, sin_t, seq)
        m_out = _mlstm(qk, mv, mo, gates, qk_conv_w[l], qk_conv_b[l][None, :], gb,
                       mh_norm_g[l][None, :], batch, seq)
        a_out = _swa(attn_sinks[l], aq, ak, av, batch, seq)
        x1, h2 = _outproj(m_out, a_out, wm, wa, x2, g_post_mix[l][None, :], g_pre_ffn[l][None, :])
        x2 = _ffn(h2, x1, wg, wv, wd, cg, cv, g_post_ffn[l][None, :], seq)
    return x2.reshape(batch, seq, D_MODEL)
```

```python
import functools

import jax
import jax.numpy as jnp
from jax import lax
from jax.experimental import pallas as pl
from jax.experimental.pallas import tpu as pltpu

F32 = jnp.float32
BF16 = jnp.bfloat16

D_MODEL = 1024
MLSTM_HEADS = 4
MLSTM_QK_DIM = 64
MLSTM_V_DIM = 128
MLSTM_QK_W = MLSTM_HEADS * MLSTM_QK_DIM
MLSTM_W = MLSTM_HEADS * MLSTM_V_DIM
CHUNK = 128
QK_CONV = 4
ATTN_HEADS = 8
ATTN_KV_HEADS = 2
ATTN_HEAD_DIM = 64
ATTN_W = ATTN_HEADS * ATTN_HEAD_DIM
ATTN_KV_W = ATTN_KV_HEADS * ATTN_HEAD_DIM
WINDOW = 128
ROPE_THETA = 10000.0
D_FF = 2816
FFN_CONV = 3
EPS = 1e-6
LOG2E = 1.4426950408889634

LANES = 128
SUBLANES = 8
GATE_W = LANES
IN_COLS = 2 * MLSTM_QK_W + 2 * MLSTM_W + ATTN_W + 2 * ATTN_KV_W + GATE_W

TM_IN, TS_IN = 512, 512
TM_OUT, TS_OUT = 1024, 256
TB_MLSTM = 512
TQ_SWA = 512
TM_FFN = 1024
TF_FFN = 256
VMEM_LIMIT = 56 * 1024 * 1024


def _sigmoid(x):
    return 1.0 / (1.0 + jnp.exp(-x))


def _log_sigmoid(x):
    return jnp.minimum(x, 0.0) - jnp.log1p(jnp.exp(-jnp.abs(x)))


def _rms(x, g):
    return x * lax.rsqrt(jnp.mean(x * x, axis=-1, keepdims=True) + EPS) * g


def _split3(x):
    hi = x.astype(BF16)
    r1 = x - hi.astype(F32)
    mid = r1.astype(BF16)
    lo = (r1 - mid.astype(F32)).astype(BF16)
    return hi, mid, lo


def _inproj_kernel(x_ref, g_ref, w_ref, cos_ref, sin_ref, cw_ref, cb_ref, gb_ref, ng_ref,
                   q_ref, kt_ref, mv_ref, og_ref, gt_ref, aq_ref, akv_ref, hist_ref, *, tiles_per_seq, nsub):
    ts = x_ref.shape[0] // nsub
    qkw = 2 * MLSTM_QK_W
    o_qk = 0
    o_v = o_qk + qkw
    o_o = o_v + MLSTM_W
    o_aq = o_o + MLSTM_W
    o_kv = o_aq + ATTN_W
    o_g = o_kv + 2 * ATTN_KV_W
    seq_start = (pl.program_id(0) % tiles_per_seq) == 0

    r_i = lax.broadcasted_iota(jnp.int32, (CHUNK, CHUNK), 0)
    c_i = lax.broadcasted_iota(jnp.int32, (CHUNK, CHUNK), 1)
    triu = (r_i <= c_i).astype(BF16)
    row8 = lax.broadcasted_iota(jnp.int32, (SUBLANES, ts), 0)
    sub = lax.broadcasted_iota(jnp.int32, (1, SUBLANES, qkw), 1)
    lane = lax.broadcasted_iota(jnp.int32, (ts, LANES), 1)
    first_half = (lane % ATTN_HEAD_DIM) < (ATTN_HEAD_DIM // 2)

    prev = jnp.where(seq_start, 0.0, hist_ref[...])
    for sb in range(nsub):
        rows = slice(sb * ts, (sb + 1) * ts)
        h = _rms(x_ref[rows, :], g_ref[...]).astype(BF16)

        def proj(lo, width):
            return jnp.dot(h, w_ref[:, lo:lo + width], preferred_element_type=F32)

        qk_pre = proj(o_qk, qkw)
        x3 = jnp.concatenate([prev[None], qk_pre.reshape(ts // SUBLANES, SUBLANES, qkw)], axis=0)
        prev = qk_pre[ts - SUBLANES:, :]
        cur, prv = x3[1:], x3[:-1]
        y = cb_ref[...][None] + cw_ref[QK_CONV - 1:QK_CONV, :][None] * cur
        for j in range(1, QK_CONV):
            shifted = pltpu.roll(jnp.where(sub >= SUBLANES - j, prv, cur), j, axis=1)
            y = y + cw_ref[QK_CONV - 1 - j:QK_CONV - j, :][None] * shifted
        y = y.reshape(ts, qkw)
        act = y * _sigmoid(y)
        q_ref[rows, :] = act[:, :MLSTM_QK_W].astype(BF16)
        kt_ref[:, rows] = (act[:, MLSTM_QK_W:] * (MLSTM_QK_DIM ** -0.5)).T.astype(BF16)

        og_ref[rows, :] = (ng_ref[...] * _sigmoid(proj(o_o, MLSTM_W))).astype(BF16)
        mv_ref[rows, :] = proj(o_v, MLSTM_W).astype(BF16)

        cos = cos_ref[rows, :]
        sin = sin_ref[rows, :]

        def rope(t):
            partner = jnp.where(first_half,
                                pltpu.roll(t, LANES - ATTN_HEAD_DIM // 2, axis=1),
                                pltpu.roll(t, ATTN_HEAD_DIM // 2, axis=1))
            return t * cos + partner * sin

        aq = proj(o_aq, ATTN_W)
        scale = ATTN_HEAD_DIM ** -0.5 * LOG2E
        for c in range(ATTN_W // LANES):
            sl = slice(c * LANES, (c + 1) * LANES)
            aq_ref[rows, sl] = (rope(aq[:, sl]) * scale).astype(BF16)
        akv = proj(o_kv, 2 * ATTN_KV_W)
        akv_ref[rows, :ATTN_KV_W] = rope(akv[:, :ATTN_KV_W]).astype(BF16)
        akv_ref[rows, ATTN_KV_W:] = akv[:, ATTN_KV_W:].astype(BF16)

        gates = proj(o_g, GATE_W) + gb_ref[...]
        gt = gates.T[0:SUBLANES, :]
        comb = jnp.where(row8 < MLSTM_HEADS, gt, _log_sigmoid(gt))
        parts = _split3(comb)
        cums = []
        for c in range(ts // CHUNK):
            cs = slice(c * CHUNK, (c + 1) * CHUNK)
            acc = jnp.zeros((SUBLANES, CHUNK), F32)
            for part in parts:
                acc = acc + jnp.dot(part[:, cs], triu, preferred_element_type=F32)
            cums.append(acc)
        b = jnp.concatenate(cums, axis=1)
        gt_ref[:, rows] = jnp.where(row8 < MLSTM_HEADS, comb - pltpu.roll(b, MLSTM_HEADS, axis=0), b)
    hist_ref[...] = prev


def _inproj(x2, g, w, cos_t, sin_t, cw, cb, gb, ng, seq, l):
    T = x2.shape[0]
    tm = min(TM_IN, seq)
    nseq = seq // tm
    row = lambda i: (i, 0)
    col = lambda i: (0, i)
    pos = lambda i: (i % nseq, 0)
    layer = lambda i: (l, 0, 0)
    return pl.pallas_call(
        functools.partial(_inproj_kernel, tiles_per_seq=nseq, nsub=tm // min(TS_IN, tm)),
        grid=(T // tm,),
        in_specs=[
            pl.BlockSpec((tm, D_MODEL), row),
            pl.BlockSpec((None, 1, D_MODEL), layer),
            pl.BlockSpec((None, D_MODEL, IN_COLS), layer),
            pl.BlockSpec((tm, LANES), pos),
            pl.BlockSpec((tm, LANES), pos),
            pl.BlockSpec((None, QK_CONV, 2 * MLSTM_QK_W), layer),
            pl.BlockSpec((None, 1, 2 * MLSTM_QK_W), layer),
            pl.BlockSpec((None, 1, GATE_W), layer),
            pl.BlockSpec((None, 1, MLSTM_W), layer),
        ],
        out_specs=[
            pl.BlockSpec((tm, MLSTM_QK_W), row),
            pl.BlockSpec((MLSTM_QK_W, tm), col),
            pl.BlockSpec((tm, MLSTM_W), row),
            pl.BlockSpec((tm, MLSTM_W), row),
            pl.BlockSpec((SUBLANES, tm), col),
            pl.BlockSpec((tm, ATTN_W), row),
            pl.BlockSpec((tm, 2 * ATTN_KV_W), row),
        ],
        out_shape=[
            jax.ShapeDtypeStruct((T, MLSTM_QK_W), BF16),
            jax.ShapeDtypeStruct((MLSTM_QK_W, T), BF16),
            jax.ShapeDtypeStruct((T, MLSTM_W), BF16),
            jax.ShapeDtypeStruct((T, MLSTM_W), BF16),
            jax.ShapeDtypeStruct((SUBLANES, T), F32),
            jax.ShapeDtypeStruct((T, ATTN_W), BF16),
            jax.ShapeDtypeStruct((T, 2 * ATTN_KV_W), BF16),
        ],
        scratch_shapes=[pltpu.VMEM((SUBLANES, 2 * MLSTM_QK_W), F32)],
        compiler_params=pltpu.CompilerParams(
            dimension_semantics=("arbitrary",), vmem_limit_bytes=VMEM_LIMIT),
        name="inproj",
    )(x2, g, w, cos_t, sin_t, cw, cb, gb, ng)


def _mlstm_kernel(q_ref, kt_ref, v_ref, og_ref, gt_ref, out_ref, c_ref, m_ref, *, nchunk):
    L = CHUNK
    H = MLSTM_HEADS
    DK = MLSTM_QK_DIM
    DV = MLSTM_V_DIM

    @pl.when(pl.program_id(1) == 0)
    def _():
        c_ref[...] = jnp.zeros_like(c_ref)
        m_ref[...] = jnp.zeros_like(m_ref)

    row_i = lax.broadcasted_iota(jnp.int32, (L, L), 0)
    col_i = lax.broadcasted_iota(jnp.int32, (L, L), 1)
    causal = col_i <= row_i
    eye = col_i == row_i
    qlane = lax.broadcasted_iota(jnp.int32, (L, H * DK), 1)
    row8 = lax.broadcasted_iota(jnp.int32, (SUBLANES, LANES), 0)
    ones_blk = jnp.ones((L, DV), BF16)
    neg_inf = jnp.float32(-jnp.inf)
    zero_bf = jnp.zeros((), BF16)

    for c in range(nchunk):
        rows = slice(c * L, (c + 1) * L)
        q = q_ref[rows, :]
        kt = kt_ref[:, rows]
        g8 = gt_ref[:, rows] * LOG2E
        m8 = m_ref[...]

        a_last8 = jnp.maximum(jnp.max(g8, axis=-1, keepdims=True), m8)
        decay8 = jnp.exp2(m8 - a_last8)
        ws8 = jnp.exp2(g8 - a_last8)
        b_last8 = pltpu.roll(g8, H, axis=0)[:, L - 1:L]
        m_ref[...] = jnp.where(row8 < H, b_last8 + a_last8, 0.0)

        qm = jnp.concatenate(
            [jnp.where((qlane >= h * DK) & (qlane < (h + 1) * DK), q, zero_bf) for h in range(H)], axis=0)
        sc = jnp.dot(qm, kt, preferred_element_type=F32)
        qc = jnp.dot(qm, c_ref[...].astype(BF16), preferred_element_type=F32)

        for h in range(H):
            hr = slice(h * L, (h + 1) * L)
            vs = slice(h * DV, (h + 1) * DV)
            ks = slice(h * DK, (h + 1) * DK)
            r_b = jnp.broadcast_to(g8[h:h + 1, :], (L, L))
            b_b = jnp.broadcast_to(g8[H + h:H + h + 1, :], (L, L))
            m_b = jnp.broadcast_to(m8[h:h + 1, :], (L, LANES))

            rmat = jnp.where(causal, r_b, neg_inf)
            a = jnp.maximum(jnp.max(rmat, axis=-1, keepdims=True), m_b)
            w_intra = jnp.exp2(rmat - a)
            w_inter = jnp.exp2(m_b - a)
            b_col = jnp.sum(jnp.where(eye, b_b, 0.0), axis=-1, keepdims=True)
            floor = jnp.exp2(-(b_col + a))

            s = (sc[hr, :] * w_intra).astype(BF16)
            v_ext = jnp.concatenate([v_ref[rows, vs], ones_blk], axis=-1)
            kw_t = (kt[ks, :].astype(F32) * ws8[h:h + 1, :]).astype(BF16)
            res = jnp.dot(jnp.concatenate([s, kw_t], axis=0), v_ext,
                          preferred_element_type=F32)
            num = w_inter * qc[hr, :DV] + res[:L, :DV]
            den = w_inter * qc[hr, DV:] + res[:L, DV:]
            hh = num / jnp.maximum(jnp.abs(den), floor)
            hn = hh * lax.rsqrt(jnp.mean(hh * hh, axis=-1, keepdims=True) + EPS)
            out_ref[rows, vs] = (hn * og_ref[rows, vs].astype(F32)).astype(BF16)

            dec = jnp.broadcast_to(decay8[h:h + 1, :], (DK, LANES))
            c_ref[ks, :] = jnp.concatenate([dec, dec], axis=-1) * c_ref[ks, :] + res[L:, :]


def _mlstm(q, kt, mv, og, gt, batch, seq):
    T = q.shape[0]
    tb = min(TB_MLSTM, seq)
    nb = seq // tb
    row = lambda b, i: (b * nb + i, 0)
    col = lambda b, i: (0, b * nb + i)
    return pl.pallas_call(
        functools.partial(_mlstm_kernel, nchunk=tb // CHUNK),
        grid=(batch, nb),
        in_specs=[
            pl.BlockSpec((tb, MLSTM_QK_W), row),
            pl.BlockSpec((MLSTM_QK_W, tb), col),
            pl.BlockSpec((tb, MLSTM_W), row),
            pl.BlockSpec((tb, MLSTM_W), row),
            pl.BlockSpec((SUBLANES, tb), col),
        ],
        out_specs=pl.BlockSpec((tb, MLSTM_W), row),
        out_shape=jax.ShapeDtypeStruct((T, MLSTM_W), BF16),
        scratch_shapes=[
            pltpu.VMEM((MLSTM_HEADS * MLSTM_QK_DIM, 2 * MLSTM_V_DIM), F32),
            pltpu.VMEM((SUBLANES, LANES), F32),
        ],
        compiler_params=pltpu.CompilerParams(
            dimension_semantics=("arbitrary", "arbitrary"), vmem_limit_bytes=VMEM_LIMIT),
        name="mlstm",
    )(q, kt, mv, og, gt)


def _swa_kernel(sink_ref, q_ref, kv_ref, kvp_ref, out_ref, *, nblk, l):
    W = WINDOW
    G = ATTN_HEADS // ATTN_KV_HEADS
    KW = ATTN_KV_W
    half = LANES // 2
    lane = lax.broadcasted_iota(jnp.int32, (W, LANES), 1)
    left = lane < half
    qpos = lax.broadcasted_iota(jnp.int32, (W, 2 * W), 0)
    kpos = lax.broadcasted_iota(jnp.int32, (W, 2 * W), 1)
    band = (kpos > qpos) & (kpos <= qpos + W)
    not_first = pl.program_id(1) > 0
    neg_inf = jnp.float32(-jnp.inf)
    zero = jnp.zeros((), BF16)

    for j in range(nblk):
        rows = slice(j * W, (j + 1) * W)
        if j == 0:
            kv2 = jnp.concatenate([kvp_ref[...], kv_ref[rows, :]], axis=0)
            valid = band & ((kpos >= W) | not_first)
        else:
            kv2 = kv_ref[(j - 1) * W:(j + 1) * W, :]
            valid = band
        k2 = kv2[:, :KW]
        v2 = kv2[:, KW:]
        qb = q_ref[rows, :]
        qs = jnp.concatenate(
            [jnp.where(left, qb[:, c * LANES:(c + 1) * LANES], zero) for c in range(G)]
            + [jnp.where(left, zero, qb[:, c * LANES:(c + 1) * LANES]) for c in range(G)], axis=0)
        s_all = lax.dot_general(qs, k2, (((1,), (1,)), ((), ())),
                                preferred_element_type=F32)
        ps, invs = [], []
        for h in range(ATTN_HEADS):
            sink = sink_ref[l, h] * LOG2E
            s = jnp.where(valid, s_all[h * W:(h + 1) * W, :], neg_inf)
            mx = jnp.maximum(jnp.max(s, axis=-1, keepdims=True), sink)
            p = jnp.exp2(s - mx)
            denom = jnp.sum(p, axis=-1, keepdims=True) + jnp.exp2(sink - mx)
            ps.append(p.astype(BF16))
            invs.append(1.0 / denom)
        pv = jnp.dot(jnp.concatenate(ps, axis=0), v2, preferred_element_type=F32)
        for c in range(G):
            lo = pv[c * W:(c + 1) * W, :] * invs[c]
            hi = pv[(G + c) * W:(G + c + 1) * W, :] * invs[G + c]
            out_ref[rows, c * LANES:(c + 1) * LANES] = jnp.where(left, lo, hi).astype(BF16)


def _swa(sinks, aq, akv, batch, seq, l):
    T = aq.shape[0]
    tq = min(TQ_SWA, seq)
    nb = seq // tq
    nblk = tq // WINDOW
    nwin = seq // WINDOW
    row = lambda b, i: (b * nb + i, 0)
    prev = lambda b, i: (b * nwin + jnp.maximum(i * nblk - 1, 0), 0)
    return pl.pallas_call(
        functools.partial(_swa_kernel, nblk=nblk, l=l),
        grid=(batch, nb),
        in_specs=[
            pl.BlockSpec(memory_space=pltpu.SMEM),
            pl.BlockSpec((tq, ATTN_W), row),
            pl.BlockSpec((tq, 2 * ATTN_KV_W), row),
            pl.BlockSpec((WINDOW, 2 * ATTN_KV_W), prev),
        ],
        out_specs=pl.BlockSpec((tq, ATTN_W), row),
        out_shape=jax.ShapeDtypeStruct((T, ATTN_W), BF16),
        compiler_params=pltpu.CompilerParams(
            dimension_semantics=("arbitrary", "arbitrary"), vmem_limit_bytes=VMEM_LIMIT),
        name="swa",
    )(sinks, aq, akv, akv)


def _outproj_kernel(m_ref, a_ref, w_ref, x_ref, gpost_ref, gpre_ref, x1_ref, h2_ref, *, nsub):
    ts = x_ref.shape[0] // nsub
    for sb in range(nsub):
        rows = slice(sb * ts, (sb + 1) * ts)
        y = (jnp.dot(m_ref[rows, :], w_ref[:MLSTM_W, :], preferred_element_type=F32)
             + jnp.dot(a_ref[rows, :], w_ref[MLSTM_W:, :], preferred_element_type=F32))
        x1 = x_ref[rows, :] + _rms(y, gpost_ref[...])
        x1_ref[rows, :] = x1
        h2_ref[rows, :] = _rms(x1, gpre_ref[...]).astype(BF16)


def _outproj(m_out, a_out, w, x2, gpost, gpre, l):
    T = x2.shape[0]
    tm = min(TM_OUT, T)
    row = lambda i: (i, 0)
    layer = lambda i: (l, 0, 0)
    return pl.pallas_call(
        functools.partial(_outproj_kernel, nsub=tm // min(TS_OUT, tm)),
        grid=(T // tm,),
        in_specs=[
            pl.BlockSpec((tm, MLSTM_W), row),
            pl.BlockSpec((tm, ATTN_W), row),
            pl.BlockSpec((None, MLSTM_W + ATTN_W, D_MODEL), layer),
            pl.BlockSpec((tm, D_MODEL), row),
            pl.BlockSpec((None, 1, D_MODEL), layer),
            pl.BlockSpec((None, 1, D_MODEL), layer),
        ],
        out_specs=[pl.BlockSpec((tm, D_MODEL), row), pl.BlockSpec((tm, D_MODEL), row)],
        out_shape=[jax.ShapeDtypeStruct((T, D_MODEL), F32), jax.ShapeDtypeStruct((T, D_MODEL), BF16)],
        compiler_params=pltpu.CompilerParams(
            dimension_semantics=("arbitrary",), vmem_limit_bytes=VMEM_LIMIT),
        name="outproj",
    )(m_out, a_out, w, x2, gpost, gpre)


def _ffn_kernel(h_ref, x_ref, wu_ref, wd_ref, cw_ref, gpost_ref, out_ref,
                hs_ref, act_ref, ua_ref, ub_ref, carry_g_ref, carry_v_ref, *, nf, tf, tiles_per_seq):
    tm = h_ref.shape[0]
    hs_ref[...] = h_ref[...]
    seq_start = (pl.program_id(0) % tiles_per_seq) == 0
    sub = lax.broadcasted_iota(jnp.int32, (1, SUBLANES, tf), 1)

    def conv(u, prev, cw):
        u3 = jnp.concatenate([prev[None], u.reshape(tm // SUBLANES, SUBLANES, tf)], axis=0)
        cur, prv = u3[1:], u3[:-1]
        s1 = pltpu.roll(jnp.where(sub >= SUBLANES - 1, prv, cur), 1, axis=1)
        s2 = pltpu.roll(jnp.where(sub >= SUBLANES - 2, prv, cur), 2, axis=1)
        y = cw[2:3, :][None] * cur + cw[1:2, :][None] * s1 + cw[0:1, :][None] * s2
        return y.reshape(tm, tf)

    def gate_cols(f):
        return pl.ds(pl.multiple_of(f * tf, tf), tf)

    def val_cols(f):
        return pl.ds(pl.multiple_of(nf * tf + f * tf, tf), tf)

    def up(f, u_ref):
        h = hs_ref[...]
        u_ref[:, :tf] = jnp.dot(h, wu_ref[:, gate_cols(f)], preferred_element_type=F32)
        u_ref[:, tf:] = jnp.dot(h, wu_ref[:, val_cols(f)], preferred_element_type=F32)

    def gate_act(f, u_ref):
        ug = u_ref[:, :tf]
        uv = u_ref[:, tf:]
        pg = jnp.where(seq_start, 0.0, carry_g_ref[f])
        pv = jnp.where(seq_start, 0.0, carry_v_ref[f])
        carry_g_ref[f] = ug[tm - SUBLANES:, :]
        carry_v_ref[f] = uv[tm - SUBLANES:, :]
        gate = conv(ug, pg, cw_ref[:, gate_cols(f)])
        val = conv(uv, pv, cw_ref[:, val_cols(f)])
        act_ref[:, gate_cols(f)] = (gate * _sigmoid(gate) * val).astype(BF16)

    up(0, ua_ref)

    def body(i, carry):
        f = 2 * i
        up(f + 1, ub_ref)
        gate_act(f, ua_ref)
        up(f + 2, ua_ref)
        gate_act(f + 1, ub_ref)
        return carry

    lax.fori_loop(0, (nf - 1) // 2, body, 0)
    gate_act(nf - 1, ua_ref)

    y = jnp.dot(act_ref[...], wd_ref[...], preferred_element_type=F32)
    out_ref[...] = x_ref[...] + _rms(y, gpost_ref[...])


def _ffn(h2, x1, wu, wd, cw, gpost, seq, l):
    T = h2.shape[0]
    tm = min(TM_FFN, seq)
    tf = TF_FFN
    nf = D_FF // tf
    assert nf % 2 == 1 and nf * tf == D_FF
    row = lambda i: (i, 0)
    layer = lambda i: (l, 0, 0)
    return pl.pallas_call(
        functools.partial(_ffn_kernel, nf=nf, tf=tf, tiles_per_seq=seq // tm),
        grid=(T // tm,),
        in_specs=[
            pl.BlockSpec((tm, D_MODEL), row),
            pl.BlockSpec((tm, D_MODEL), row),
            pl.BlockSpec((None, D_MODEL, 2 * D_FF), layer),
            pl.BlockSpec((None, D_FF, D_MODEL), layer),
            pl.BlockSpec((None, FFN_CONV, 2 * D_FF), layer),
            pl.BlockSpec((None, 1, D_MODEL), layer),
        ],
        out_specs=pl.BlockSpec((tm, D_MODEL), row),
        out_shape=jax.ShapeDtypeStruct((T, D_MODEL), F32),
        scratch_shapes=[
            pltpu.VMEM((tm, D_MODEL), BF16),
            pltpu.VMEM((tm, D_FF), BF16),
            pltpu.VMEM((tm, 2 * tf), F32),
            pltpu.VMEM((tm, 2 * tf), F32),
            pltpu.VMEM((nf, SUBLANES, tf), F32),
            pltpu.VMEM((nf, SUBLANES, tf), F32),
        ],
        compiler_params=pltpu.CompilerParams(
            dimension_semantics=("arbitrary",), vmem_limit_bytes=VMEM_LIMIT),
        name="ffn",
    )(h2, x1, wu, wd, cw, gpost)


def _pair_heads(a, axis):
    G = ATTN_HEADS // ATTN_KV_HEADS
    shape = a.shape
    a = a.reshape(shape[:axis] + (ATTN_KV_HEADS, G, ATTN_HEAD_DIM) + shape[axis + 1:])
    a = jnp.swapaxes(a, axis, axis + 1)
    return a.reshape(shape)


def _rope_tables(seq):
    inv = 1.0 / (ROPE_THETA ** (jnp.arange(0, ATTN_HEAD_DIM, 2, dtype=F32) / ATTN_HEAD_DIM))
    ang = jnp.arange(seq, dtype=F32)[:, None] * inv[None, :]
    cos, sin = lax.optimization_barrier((jnp.cos(ang), jnp.sin(ang)))
    reps = LANES // ATTN_HEAD_DIM
    cos_t = jnp.concatenate([cos, cos] * reps, axis=-1)
    sin_t = jnp.concatenate([-sin, sin] * reps, axis=-1)
    assert cos_t.shape == (seq, LANES)
    return cos_t, sin_t


def kernel(x, g_pre_mix, w_in, qk_conv_w, qk_conv_b, gate_bias, mh_norm_g, attn_sinks, w_out,
           g_post_mix, g_pre_ffn, w_up, ffn_conv_w, w_down, g_post_ffn):
    batch, seq, _ = x.shape
    depth = w_in.shape[0]
    T = batch * seq
    cos_t, sin_t = _rope_tables(seq)

    n_gate = 2 * MLSTM_HEADS
    o_g = 2 * MLSTM_QK_W + 2 * MLSTM_W
    o_aq = o_g + n_gate
    o_ak = o_aq + ATTN_W
    w_a = jnp.concatenate([
        w_in[:, :, :o_g],
        _pair_heads(w_in[:, :, o_aq:o_ak], 2),
        w_in[:, :, o_ak:],
        w_in[:, :, o_g:o_aq],
        jnp.zeros((depth, D_MODEL, GATE_W - n_gate), w_in.dtype),
    ], axis=2).astype(BF16)
    gb = jnp.concatenate([gate_bias, jnp.zeros((depth, GATE_W - n_gate), F32)], axis=1)[:, None, :]
    w_o = jnp.concatenate([w_out[:, :MLSTM_W], _pair_heads(w_out[:, MLSTM_W:], 1)], axis=1).astype(BF16)
    w_u = w_up.astype(BF16)
    w_d = w_down.astype(BF16)
    vec = lambda p: p[:, None, :]

    x2 = x.reshape(T, D_MODEL)
    for l in range(depth):
        q, kt, mv, og, gt, aq, akv = _inproj(
            x2, vec(g_pre_mix), w_a, cos_t, sin_t, qk_conv_w, vec(qk_conv_b), gb, vec(mh_norm_g), seq, l)
        m_out = _mlstm(q, kt, mv, og, gt, batch, seq)
        a_out = _swa(attn_sinks, aq, akv, batch, seq, l)
        x1, h2 = _outproj(m_out, a_out, w_o, x2, vec(g_post_mix), vec(g_pre_ffn), l)
        x2 = _ffn(h2, x1, w_u, w_d, ffn_conv_w, vec(g_post_ffn), seq, l)
    return x2.reshape(batch, seq, D_MODEL)
```

```python
import functools

import jax
import jax.numpy as jnp
from jax import lax
from jax.experimental import pallas as pl
from jax.experimental.pallas import tpu as pltpu

F32 = jnp.float32
BF16 = jnp.bfloat16

D_MODEL = 1024
MLSTM_HEADS = 4
MLSTM_QK_DIM = 64
MLSTM_V_DIM = 128
MLSTM_QK_W = MLSTM_HEADS * MLSTM_QK_DIM
MLSTM_W = MLSTM_HEADS * MLSTM_V_DIM
CHUNK = 128
QK_CONV = 4
ATTN_HEADS = 8
ATTN_KV_HEADS = 2
ATTN_HEAD_DIM = 64
ATTN_W = ATTN_HEADS * ATTN_HEAD_DIM
ATTN_KV_W = ATTN_KV_HEADS * ATTN_HEAD_DIM
WINDOW = 128
ROPE_THETA = 10000.0
D_FF = 2816
FFN_CONV = 3
EPS = 1e-6
LOG2E = 1.4426950408889634

LANES = 128
SUBLANES = 8
GATE_W = LANES
IN_COLS = 2 * MLSTM_QK_W + 2 * MLSTM_W + ATTN_W + 2 * ATTN_KV_W + GATE_W

TM_IN, TS_IN = 512, 512
TM_OUT, TS_OUT = 1024, 256
TB_MLSTM = 512
TQ_SWA = 512
TM_FFN = 1024
TF_FFN = 256
TR_FFN = 128
VMEM_LIMIT = 56 * 1024 * 1024


def _sigmoid(x):
    return 1.0 / (1.0 + jnp.exp(-x))


def _log_sigmoid(x):
    return jnp.minimum(x, 0.0) - jnp.log1p(jnp.exp(-jnp.abs(x)))


def _rms(x, g):
    return x * lax.rsqrt(jnp.mean(x * x, axis=-1, keepdims=True) + EPS) * g


def _split3(x):
    hi = x.astype(BF16)
    r1 = x - hi.astype(F32)
    mid = r1.astype(BF16)
    lo = (r1 - mid.astype(F32)).astype(BF16)
    return hi, mid, lo


def _inproj_kernel(x_ref, g_ref, w_ref, cos_ref, sin_ref, cw_ref, cb_ref, gb_ref, ng_ref,
                   q_ref, kt_ref, mv_ref, og_ref, gt_ref, aq_ref, akv_ref, hist_ref, *, tiles_per_seq, nsub):
    ts = x_ref.shape[0] // nsub
    qkw = 2 * MLSTM_QK_W
    o_qk = 0
    o_v = o_qk + qkw
    o_o = o_v + MLSTM_W
    o_aq = o_o + MLSTM_W
    o_kv = o_aq + ATTN_W
    o_g = o_kv + 2 * ATTN_KV_W
    seq_start = (pl.program_id(0) % tiles_per_seq) == 0

    r_i = lax.broadcasted_iota(jnp.int32, (CHUNK, CHUNK), 0)
    c_i = lax.broadcasted_iota(jnp.int32, (CHUNK, CHUNK), 1)
    triu = (r_i <= c_i).astype(BF16)
    row8 = lax.broadcasted_iota(jnp.int32, (SUBLANES, ts), 0)
    sub = lax.broadcasted_iota(jnp.int32, (1, SUBLANES, qkw), 1)
    lane = lax.broadcasted_iota(jnp.int32, (ts, LANES), 1)
    first_half = (lane % ATTN_HEAD_DIM) < (ATTN_HEAD_DIM // 2)

    prev = jnp.where(seq_start, 0.0, hist_ref[...])
    for sb in range(nsub):
        rows = slice(sb * ts, (sb + 1) * ts)
        h = _rms(x_ref[rows, :], g_ref[...]).astype(BF16)

        def proj(lo, width):
            return jnp.dot(h, w_ref[:, lo:lo + width], preferred_element_type=F32)

        qk_pre = proj(o_qk, qkw)
        x3 = jnp.concatenate([prev[None], qk_pre.reshape(ts // SUBLANES, SUBLANES, qkw)], axis=0)
        prev = qk_pre[ts - SUBLANES:, :]
        cur, prv = x3[1:], x3[:-1]
        y = cb_ref[...][None] + cw_ref[QK_CONV - 1:QK_CONV, :][None] * cur
        for j in range(1, QK_CONV):
            shifted = pltpu.roll(jnp.where(sub >= SUBLANES - j, prv, cur), j, axis=1)
            y = y + cw_ref[QK_CONV - 1 - j:QK_CONV - j, :][None] * shifted
        y = y.reshape(ts, qkw)
        act = y * _sigmoid(y)
        q_ref[rows, :] = act[:, :MLSTM_QK_W].astype(BF16)
        kt_ref[:, rows] = (act[:, MLSTM_QK_W:] * (MLSTM_QK_DIM ** -0.5)).T.astype(BF16)

        og_ref[rows, :] = (ng_ref[...] * _sigmoid(proj(o_o, MLSTM_W))).astype(BF16)
        mv_ref[rows, :] = proj(o_v, MLSTM_W).astype(BF16)

        cos = cos_ref[rows, :]
        sin = sin_ref[rows, :]

        def rope(t):
            partner = jnp.where(first_half,
                                pltpu.roll(t, LANES - ATTN_HEAD_DIM // 2, axis=1),
                                pltpu.roll(t, ATTN_HEAD_DIM // 2, axis=1))
            return t * cos + partner * sin

        aq = proj(o_aq, ATTN_W)
        scale = ATTN_HEAD_DIM ** -0.5 * LOG2E
        for c in range(ATTN_W // LANES):
            sl = slice(c * LANES, (c + 1) * LANES)
            aq_ref[rows, sl] = (rope(aq[:, sl]) * scale).astype(BF16)
        akv = proj(o_kv, 2 * ATTN_KV_W)
        akv_ref[rows, :ATTN_KV_W] = rope(akv[:, :ATTN_KV_W]).astype(BF16)
        akv_ref[rows, ATTN_KV_W:] = akv[:, ATTN_KV_W:].astype(BF16)

        gates = proj(o_g, GATE_W) + gb_ref[...]
        gt = gates.T[0:SUBLANES, :]
        comb = jnp.where(row8 < MLSTM_HEADS, gt, _log_sigmoid(gt))
        parts = _split3(comb)
        cums = []
        for c in range(ts // CHUNK):
            cs = slice(c * CHUNK, (c + 1) * CHUNK)
            acc = jnp.zeros((SUBLANES, CHUNK), F32)
            for part in parts:
                acc = acc + jnp.dot(part[:, cs], triu, preferred_element_type=F32)
            cums.append(acc)
        b = jnp.concatenate(cums, axis=1)
        gt_ref[:, rows] = jnp.where(row8 < MLSTM_HEADS, comb - pltpu.roll(b, MLSTM_HEADS, axis=0), b)
    hist_ref[...] = prev


def _inproj(x2, g, w, cos_t, sin_t, cw, cb, gb, ng, seq, l):
    T = x2.shape[0]
    tm = min(TM_IN, seq)
    nseq = seq // tm
    row = lambda i: (i, 0)
    col = lambda i: (0, i)
    pos = lambda i: (i % nseq, 0)
    layer = lambda i: (l, 0, 0)
    return pl.pallas_call(
        functools.partial(_inproj_kernel, tiles_per_seq=nseq, nsub=tm // min(TS_IN, tm)),
        grid=(T // tm,),
        in_specs=[
            pl.BlockSpec((tm, D_MODEL), row),
            pl.BlockSpec((None, 1, D_MODEL), layer),
            pl.BlockSpec((None, D_MODEL, IN_COLS), layer),
            pl.BlockSpec((tm, LANES), pos),
            pl.BlockSpec((tm, LANES), pos),
            pl.BlockSpec((None, QK_CONV, 2 * MLSTM_QK_W), layer),
            pl.BlockSpec((None, 1, 2 * MLSTM_QK_W), layer),
            pl.BlockSpec((None, 1, GATE_W), layer),
            pl.BlockSpec((None, 1, MLSTM_W), layer),
        ],
        out_specs=[
            pl.BlockSpec((tm, MLSTM_QK_W), row),
            pl.BlockSpec((MLSTM_QK_W, tm), col),
            pl.BlockSpec((tm, MLSTM_W), row),
            pl.BlockSpec((tm, MLSTM_W), row),
            pl.BlockSpec((SUBLANES, tm), col),
            pl.BlockSpec((tm, ATTN_W), row),
            pl.BlockSpec((tm, 2 * ATTN_KV_W), row),
        ],
        out_shape=[
            jax.ShapeDtypeStruct((T, MLSTM_QK_W), BF16),
            jax.ShapeDtypeStruct((MLSTM_QK_W, T), BF16),
            jax.ShapeDtypeStruct((T, MLSTM_W), BF16),
            jax.ShapeDtypeStruct((T, MLSTM_W), BF16),
            jax.ShapeDtypeStruct((SUBLANES, T), F32),
            jax.ShapeDtypeStruct((T, ATTN_W), BF16),
            jax.ShapeDtypeStruct((T, 2 * ATTN_KV_W), BF16),
        ],
        scratch_shapes=[pltpu.VMEM((SUBLANES, 2 * MLSTM_QK_W), F32)],
        compiler_params=pltpu.CompilerParams(
            dimension_semantics=("arbitrary",), vmem_limit_bytes=VMEM_LIMIT),
        name="inproj",
    )(x2, g, w, cos_t, sin_t, cw, cb, gb, ng)


def _mlstm_kernel(q_ref, kt_ref, v_ref, og_ref, gt_ref, out_ref, c_ref, m_ref, *, nchunk):
    L = CHUNK
    H = MLSTM_HEADS
    DK = MLSTM_QK_DIM
    DV = MLSTM_V_DIM

    @pl.when(pl.program_id(1) == 0)
    def _():
        c_ref[...] = jnp.zeros_like(c_ref)
        m_ref[...] = jnp.zeros_like(m_ref)

    row_i = lax.broadcasted_iota(jnp.int32, (L, L), 0)
    col_i = lax.broadcasted_iota(jnp.int32, (L, L), 1)
    causal = col_i <= row_i
    eye = col_i == row_i
    qlane = lax.broadcasted_iota(jnp.int32, (L, H * DK), 1)
    row8 = lax.broadcasted_iota(jnp.int32, (SUBLANES, LANES), 0)
    ones_blk = jnp.ones((L, DV), BF16)
    neg_inf = jnp.float32(-jnp.inf)
    zero_bf = jnp.zeros((), BF16)

    for c in range(nchunk):
        rows = slice(c * L, (c + 1) * L)
        q = q_ref[rows, :]
        kt = kt_ref[:, rows]
        g8 = gt_ref[:, rows] * LOG2E
        m8 = m_ref[...]

        a_last8 = jnp.maximum(jnp.max(g8, axis=-1, keepdims=True), m8)
        decay8 = jnp.exp2(m8 - a_last8)
        ws8 = jnp.exp2(g8 - a_last8)
        b_last8 = pltpu.roll(g8, H, axis=0)[:, L - 1:L]
        m_ref[...] = jnp.where(row8 < H, b_last8 + a_last8, 0.0)

        qm = jnp.concatenate(
            [jnp.where((qlane >= h * DK) & (qlane < (h + 1) * DK), q, zero_bf) for h in range(H)], axis=0)
        sc = jnp.dot(qm, kt, preferred_element_type=F32)
        qc = jnp.dot(qm, c_ref[...].astype(BF16), preferred_element_type=F32)

        for h in range(H):
            hr = slice(h * L, (h + 1) * L)
            vs = slice(h * DV, (h + 1) * DV)
            ks = slice(h * DK, (h + 1) * DK)
            r_b = jnp.broadcast_to(g8[h:h + 1, :], (L, L))
            b_b = jnp.broadcast_to(g8[H + h:H + h + 1, :], (L, L))
            m_b = jnp.broadcast_to(m8[h:h + 1, :], (L, LANES))

            rmat = jnp.where(causal, r_b, neg_inf)
            a = jnp.maximum(jnp.max(rmat, axis=-1, keepdims=True), m_b)
            w_intra = jnp.exp2(rmat - a)
            w_inter = jnp.exp2(m_b - a)
            b_col = jnp.sum(jnp.where(eye, b_b, 0.0), axis=-1, keepdims=True)
            floor = jnp.exp2(-(b_col + a))

            s = (sc[hr, :] * w_intra).astype(BF16)
            v_ext = jnp.concatenate([v_ref[rows, vs], ones_blk], axis=-1)
            kw_t = (kt[ks, :].astype(F32) * ws8[h:h + 1, :]).astype(BF16)
            res = jnp.dot(jnp.concatenate([s, kw_t], axis=0), v_ext,
                          preferred_element_type=F32)
            num = w_inter * qc[hr, :DV] + res[:L, :DV]
            den = w_inter * qc[hr, DV:] + res[:L, DV:]
            hh = num / jnp.maximum(jnp.abs(den), floor)
            hn = hh * lax.rsqrt(jnp.mean(hh * hh, axis=-1, keepdims=True) + EPS)
            out_ref[rows, vs] = (hn * og_ref[rows, vs].astype(F32)).astype(BF16)

            dec = jnp.broadcast_to(decay8[h:h + 1, :], (DK, LANES))
            c_ref[ks, :] = jnp.concatenate([dec, dec], axis=-1) * c_ref[ks, :] + res[L:, :]


def _mlstm(q, kt, mv, og, gt, batch, seq):
    T = q.shape[0]
    tb = min(TB_MLSTM, seq)
    nb = seq // tb
    row = lambda b, i: (b * nb + i, 0)
    col = lambda b, i: (0, b * nb + i)
    return pl.pallas_call(
        functools.partial(_mlstm_kernel, nchunk=tb // CHUNK),
        grid=(batch, nb),
        in_specs=[
            pl.BlockSpec((tb, MLSTM_QK_W), row),
            pl.BlockSpec((MLSTM_QK_W, tb), col),
            pl.BlockSpec((tb, MLSTM_W), row),
            pl.BlockSpec((tb, MLSTM_W), row),
            pl.BlockSpec((SUBLANES, tb), col),
        ],
        out_specs=pl.BlockSpec((tb, MLSTM_W), row),
        out_shape=jax.ShapeDtypeStruct((T, MLSTM_W), BF16),
        scratch_shapes=[
            pltpu.VMEM((MLSTM_HEADS * MLSTM_QK_DIM, 2 * MLSTM_V_DIM), F32),
            pltpu.VMEM((SUBLANES, LANES), F32),
        ],
        compiler_params=pltpu.CompilerParams(
            dimension_semantics=("arbitrary", "arbitrary"), vmem_limit_bytes=VMEM_LIMIT),
        name="mlstm",
    )(q, kt, mv, og, gt)


def _swa_kernel(sink_ref, q_ref, kv_ref, kvp_ref, out_ref, *, nblk, l):
    W = WINDOW
    G = ATTN_HEADS // ATTN_KV_HEADS
    KW = ATTN_KV_W
    half = LANES // 2
    lane = lax.broadcasted_iota(jnp.int32, (W, LANES), 1)
    left = lane < half
    qpos = lax.broadcasted_iota(jnp.int32, (W, 2 * W), 0)
    kpos = lax.broadcasted_iota(jnp.int32, (W, 2 * W), 1)
    band = (kpos > qpos) & (kpos <= qpos + W)
    not_first = pl.program_id(1) > 0
    neg_inf = jnp.float32(-jnp.inf)
    zero = jnp.zeros((), BF16)

    for j in range(nblk):
        rows = slice(j * W, (j + 1) * W)
        if j == 0:
            kv2 = jnp.concatenate([kvp_ref[...], kv_ref[rows, :]], axis=0)
            valid = band & ((kpos >= W) | not_first)
        else:
            kv2 = kv_ref[(j - 1) * W:(j + 1) * W, :]
            valid = band
        k2 = kv2[:, :KW]
        v2 = kv2[:, KW:]
        qb = q_ref[rows, :]
        qs = jnp.concatenate(
            [jnp.where(left, qb[:, c * LANES:(c + 1) * LANES], zero) for c in range(G)]
            + [jnp.where(left, zero, qb[:, c * LANES:(c + 1) * LANES]) for c in range(G)], axis=0)
        s_all = lax.dot_general(qs, k2, (((1,), (1,)), ((), ())),
                                preferred_element_type=F32)
        ps, invs = [], []
        for h in range(ATTN_HEADS):
            sink = sink_ref[l, h] * LOG2E
            s = jnp.where(valid, s_all[h * W:(h + 1) * W, :], neg_inf)
            mx = jnp.maximum(jnp.max(s, axis=-1, keepdims=True), sink)
            p = jnp.exp2(s - mx)
            denom = jnp.sum(p, axis=-1, keepdims=True) + jnp.exp2(sink - mx)
            ps.append(p.astype(BF16))
            invs.append(1.0 / denom)
        pv = jnp.dot(jnp.concatenate(ps, axis=0), v2, preferred_element_type=F32)
        for c in range(G):
            lo = pv[c * W:(c + 1) * W, :] * invs[c]
            hi = pv[(G + c) * W:(G + c + 1) * W, :] * invs[G + c]
            out_ref[rows, c * LANES:(c + 1) * LANES] = jnp.where(left, lo, hi).astype(BF16)


def _swa(sinks, aq, akv, batch, seq, l):
    T = aq.shape[0]
    tq = min(TQ_SWA, seq)
    nb = seq // tq
    nblk = tq // WINDOW
    nwin = seq // WINDOW
    row = lambda b, i: (b * nb + i, 0)
    prev = lambda b, i: (b * nwin + jnp.maximum(i * nblk - 1, 0), 0)
    return pl.pallas_call(
        functools.partial(_swa_kernel, nblk=nblk, l=l),
        grid=(batch, nb),
        in_specs=[
            pl.BlockSpec(memory_space=pltpu.SMEM),
            pl.BlockSpec((tq, ATTN_W), row),
            pl.BlockSpec((tq, 2 * ATTN_KV_W), row),
            pl.BlockSpec((WINDOW, 2 * ATTN_KV_W), prev),
        ],
        out_specs=pl.BlockSpec((tq, ATTN_W), row),
        out_shape=jax.ShapeDtypeStruct((T, ATTN_W), BF16),
        compiler_params=pltpu.CompilerParams(
            dimension_semantics=("arbitrary", "arbitrary"), vmem_limit_bytes=VMEM_LIMIT),
        name="swa",
    )(sinks, aq, akv, akv)


def _outproj_kernel(m_ref, a_ref, w_ref, x_ref, gpost_ref, gpre_ref, x1_ref, h2_ref, *, nsub):
    ts = x_ref.shape[0] // nsub
    for sb in range(nsub):
        rows = slice(sb * ts, (sb + 1) * ts)
        y = (jnp.dot(m_ref[rows, :], w_ref[:MLSTM_W, :], preferred_element_type=F32)
             + jnp.dot(a_ref[rows, :], w_ref[MLSTM_W:, :], preferred_element_type=F32))
        x1 = x_ref[rows, :] + _rms(y, gpost_ref[...])
        x1_ref[rows, :] = x1
        h2_ref[rows, :] = _rms(x1, gpre_ref[...]).astype(BF16)


def _outproj(m_out, a_out, w, x2, gpost, gpre, l):
    T = x2.shape[0]
    tm = min(TM_OUT, T)
    row = lambda i: (i, 0)
    layer = lambda i: (l, 0, 0)
    return pl.pallas_call(
        functools.partial(_outproj_kernel, nsub=tm // min(TS_OUT, tm)),
        grid=(T // tm,),
        in_specs=[
            pl.BlockSpec((tm, MLSTM_W), row),
            pl.BlockSpec((tm, ATTN_W), row),
            pl.BlockSpec((None, MLSTM_W + ATTN_W, D_MODEL), layer),
            pl.BlockSpec((tm, D_MODEL), row),
            pl.BlockSpec((None, 1, D_MODEL), layer),
            pl.BlockSpec((None, 1, D_MODEL), layer),
        ],
        out_specs=[pl.BlockSpec((tm, D_MODEL), row), pl.BlockSpec((tm, D_MODEL), row)],
        out_shape=[jax.ShapeDtypeStruct((T, D_MODEL), F32), jax.ShapeDtypeStruct((T, D_MODEL), BF16)],
        compiler_params=pltpu.CompilerParams(
            dimension_semantics=("arbitrary",), vmem_limit_bytes=VMEM_LIMIT),
        name="outproj",
    )(m_out, a_out, w, x2, gpost, gpre)


def _ffn_kernel(h_ref, x_ref, wu_ref, wd_ref, cw_ref, gpost_ref, out_ref,
                hs_ref, act_ref, ua_ref, ub_ref, carry_g_ref, carry_v_ref, *, nf, tf, tiles_per_seq):
    tm = h_ref.shape[0]
    hs_ref[...] = h_ref[...]
    seq_start = (pl.program_id(0) % tiles_per_seq) == 0
    sub = lax.broadcasted_iota(jnp.int32, (1, SUBLANES, tf), 1)

    def conv(u, prev, cw):
        u3 = jnp.concatenate([prev[None], u.reshape(u.shape[0] // SUBLANES, SUBLANES, tf)], axis=0)
        cur, prv = u3[1:], u3[:-1]
        s1 = pltpu.roll(jnp.where(sub >= SUBLANES - 1, prv, cur), 1, axis=1)
        s2 = pltpu.roll(jnp.where(sub >= SUBLANES - 2, prv, cur), 2, axis=1)
        y = cw[2:3, :][None] * cur + cw[1:2, :][None] * s1 + cw[0:1, :][None] * s2
        return y.reshape(u.shape[0], tf)

    def gate_cols(f):
        return pl.ds(pl.multiple_of(f * tf, tf), tf)

    def val_cols(f):
        return pl.ds(pl.multiple_of(nf * tf + f * tf, tf), tf)

    tr = TR_FFN
    row_blocks = [slice(r * tr, (r + 1) * tr) for r in range(tm // tr)]

    def up(f, u_ref):
        for rs in row_blocks:
            h = hs_ref[rs, :]
            u_ref[rs, :tf] = jnp.dot(h, wu_ref[:, gate_cols(f)], preferred_element_type=F32)
            u_ref[rs, tf:] = jnp.dot(h, wu_ref[:, val_cols(f)], preferred_element_type=F32)

    def gate_act(f, u_ref):
        pg = jnp.where(seq_start, 0.0, carry_g_ref[f])
        pv = jnp.where(seq_start, 0.0, carry_v_ref[f])
        cwg = cw_ref[:, gate_cols(f)]
        cwv = cw_ref[:, val_cols(f)]
        for rs in row_blocks:
            ug = u_ref[rs, :tf]
            uv = u_ref[rs, tf:]
            gate = conv(ug, pg, cwg)
            val = conv(uv, pv, cwv)
            pg = ug[tr - SUBLANES:, :]
            pv = uv[tr - SUBLANES:, :]
            act_ref[rs, gate_cols(f)] = (gate * _sigmoid(gate) * val).astype(BF16)
        carry_g_ref[f] = pg
        carry_v_ref[f] = pv

    up(0, ua_ref)

    def body(i, carry):
        f = 2 * i
        up(f + 1, ub_ref)
        gate_act(f, ua_ref)
        up(f + 2, ua_ref)
        gate_act(f + 1, ub_ref)
        return carry

    lax.fori_loop(0, (nf - 1) // 2, body, 0)
    gate_act(nf - 1, ua_ref)

    y = jnp.dot(act_ref[...], wd_ref[...], preferred_element_type=F32)
    out_ref[...] = x_ref[...] + _rms(y, gpost_ref[...])


def _ffn(h2, x1, wu, wd, cw, gpost, seq, l):
    T = h2.shape[0]
    tm = min(TM_FFN, seq)
    tf = TF_FFN
    nf = D_FF // tf
    assert nf % 2 == 1 and nf * tf == D_FF
    row = lambda i: (i, 0)
    layer = lambda i: (l, 0, 0)
    return pl.pallas_call(
        functools.partial(_ffn_kernel, nf=nf, tf=tf, tiles_per_seq=seq // tm),
        grid=(T // tm,),
        in_specs=[
            pl.BlockSpec((tm, D_MODEL), row),
            pl.BlockSpec((tm, D_MODEL), row),
            pl.BlockSpec((None, D_MODEL, 2 * D_FF), layer),
            pl.BlockSpec((None, D_FF, D_MODEL), layer),
            pl.BlockSpec((None, FFN_CONV, 2 * D_FF), layer),
            pl.BlockSpec((None, 1, D_MODEL), layer),
        ],
        out_specs=pl.BlockSpec((tm, D_MODEL), row),
        out_shape=jax.ShapeDtypeStruct((T, D_MODEL), F32),
        scratch_shapes=[
            pltpu.VMEM((tm, D_MODEL), BF16),
            pltpu.VMEM((tm, D_FF), BF16),
            pltpu.VMEM((tm, 2 * tf), F32),
            pltpu.VMEM((tm, 2 * tf), F32),
            pltpu.VMEM((nf, SUBLANES, tf), F32),
            pltpu.VMEM((nf, SUBLANES, tf), F32),
        ],
        compiler_params=pltpu.CompilerParams(
            dimension_semantics=("arbitrary",), vmem_limit_bytes=VMEM_LIMIT),
        name="ffn",
    )(h2, x1, wu, wd, cw, gpost)


def _pair_heads(a, axis):
    G = ATTN_HEADS // ATTN_KV_HEADS
    shape = a.shape
    a = a.reshape(shape[:axis] + (ATTN_KV_HEADS, G, ATTN_HEAD_DIM) + shape[axis + 1:])
    a = jnp.swapaxes(a, axis, axis + 1)
    return a.reshape(shape)


def _rope_tables(seq):
    half = ATTN_HEAD_DIM // 2
    reps = LANES // half
    inv = 1.0 / (ROPE_THETA ** (jnp.arange(0, ATTN_HEAD_DIM, 2, dtype=F32) / ATTN_HEAD_DIM))
    pos = jnp.arange(seq, dtype=F32).reshape(seq // reps, reps, 1)
    ang = (pos * inv[None, None, :]).reshape(seq // reps, LANES)
    cos, sin = lax.optimization_barrier((jnp.cos(ang), jnp.sin(ang)))
    cos = cos.reshape(seq, half)
    sin = sin.reshape(seq, half)
    cos_t = jnp.concatenate([cos] * reps, axis=-1)
    sin_t = jnp.concatenate([-sin, sin] * (reps // 2), axis=-1)
    assert cos_t.shape == (seq, LANES)
    return cos_t, sin_t


def kernel(x, g_pre_mix, w_in, qk_conv_w, qk_conv_b, gate_bias, mh_norm_g, attn_sinks, w_out,
           g_post_mix, g_pre_ffn, w_up, ffn_conv_w, w_down, g_post_ffn):
    batch, seq, _ = x.shape
    depth = w_in.shape[0]
    T = batch * seq
    cos_t, sin_t = _rope_tables(seq)

    n_gate = 2 * MLSTM_HEADS
    o_g = 2 * MLSTM_QK_W + 2 * MLSTM_W
    o_aq = o_g + n_gate
    o_ak = o_aq + ATTN_W
    w_a = jnp.concatenate([
        w_in[:, :, :o_g],
        _pair_heads(w_in[:, :, o_aq:o_ak], 2),
        w_in[:, :, o_ak:],
        w_in[:, :, o_g:o_aq],
        jnp.zeros((depth, D_MODEL, GATE_W - n_gate), w_in.dtype),
    ], axis=2).astype(BF16)
    gb = jnp.concatenate([gate_bias, jnp.zeros((depth, GATE_W - n_gate), F32)], axis=1)[:, None, :]
    w_o = jnp.concatenate([w_out[:, :MLSTM_W], _pair_heads(w_out[:, MLSTM_W:], 1)], axis=1).astype(BF16)
    w_u = w_up.astype(BF16)
    w_d = w_down.astype(BF16)
    vec = lambda p: p[:, None, :]

    x2 = x.reshape(T, D_MODEL)
    for l in range(depth):
        q, kt, mv, og, gt, aq, akv = _inproj(
            x2, vec(g_pre_mix), w_a, cos_t, sin_t, qk_conv_w, vec(qk_conv_b), gb, vec(mh_norm_g), seq, l)
        m_out = _mlstm(q, kt, mv, og, gt, batch, seq)
        a_out = _swa(attn_sinks, aq, akv, batch, seq, l)
        x1, h2 = _outproj(m_out, a_out, w_o, x2, vec(g_post_mix), vec(g_pre_ffn), l)
        x2 = _ffn(h2, x1, w_u, w_d, ffn_conv_w, vec(g_post_ffn), seq, l)
    return x2.reshape(batch, seq, D_MODEL)
```

```python
import functools

import jax
import jax.numpy as jnp
from jax import lax
from jax.experimental import pallas as pl
from jax.experimental.pallas import tpu as pltpu

F32 = jnp.float32
BF16 = jnp.bfloat16

D_MODEL = 1024
MLSTM_HEADS = 4
MLSTM_QK_DIM = 64
MLSTM_V_DIM = 128
MLSTM_QK_W = MLSTM_HEADS * MLSTM_QK_DIM
MLSTM_W = MLSTM_HEADS * MLSTM_V_DIM
CHUNK = 128
QK_CONV = 4
ATTN_HEADS = 8
ATTN_KV_HEADS = 2
ATTN_HEAD_DIM = 64
ATTN_W = ATTN_HEADS * ATTN_HEAD_DIM
ATTN_KV_W = ATTN_KV_HEADS * ATTN_HEAD_DIM
WINDOW = 128
ROPE_THETA = 10000.0
D_FF = 2816
FFN_CONV = 3
EPS = 1e-6
LOG2E = 1.4426950408889634

LANES = 128
SUBLANES = 8
GATE_W = LANES
IN_COLS = 2 * MLSTM_QK_W + 2 * MLSTM_W + ATTN_W + 2 * ATTN_KV_W + GATE_W

TM_IN, TS_IN = 1024, 1024
TM_OUT, TS_OUT = 1024, 256
TB_MLSTM = 512
TQ_SWA = 512
TM_FFN = 1024
TF_FFN = 256
TR_FFN = 1024
VMEM_LIMIT = 56 * 1024 * 1024


def _sigmoid(x):
    return 1.0 / (1.0 + jnp.exp(-x))


def _log_sigmoid(x):
    return jnp.minimum(x, 0.0) - jnp.log1p(jnp.exp(-jnp.abs(x)))


def _rms(x, g):
    return x * lax.rsqrt(jnp.mean(x * x, axis=-1, keepdims=True) + EPS) * g


def _split3(x):
    hi = x.astype(BF16)
    r1 = x - hi.astype(F32)
    mid = r1.astype(BF16)
    lo = (r1 - mid.astype(F32)).astype(BF16)
    return hi, mid, lo


def _inproj_kernel(x_ref, g_ref, w_ref, cos_ref, sin_ref, cw_ref, cb_ref, gb_ref, ng_ref,
                   q_ref, kt_ref, mv_ref, og_ref, gt_ref, aq_ref, akv_ref, hist_ref, *, tiles_per_seq, nsub):
    ts = x_ref.shape[0] // nsub
    qkw = 2 * MLSTM_QK_W
    o_qk = 0
    o_v = o_qk + qkw
    o_o = o_v + MLSTM_W
    o_aq = o_o + MLSTM_W
    o_kv = o_aq + ATTN_W
    o_g = o_kv + 2 * ATTN_KV_W
    seq_start = (pl.program_id(0) % tiles_per_seq) == 0

    r_i = lax.broadcasted_iota(jnp.int32, (CHUNK, CHUNK), 0)
    c_i = lax.broadcasted_iota(jnp.int32, (CHUNK, CHUNK), 1)
    triu = (r_i <= c_i).astype(BF16)
    row8 = lax.broadcasted_iota(jnp.int32, (SUBLANES, ts), 0)
    sub = lax.broadcasted_iota(jnp.int32, (1, SUBLANES, qkw), 1)
    lane = lax.broadcasted_iota(jnp.int32, (ts, LANES), 1)
    first_half = (lane % ATTN_HEAD_DIM) < (ATTN_HEAD_DIM // 2)

    prev = jnp.where(seq_start, 0.0, hist_ref[...])
    for sb in range(nsub):
        rows = slice(sb * ts, (sb + 1) * ts)
        h = _rms(x_ref[rows, :], g_ref[...]).astype(BF16)

        def proj(lo, width):
            return jnp.dot(h, w_ref[:, lo:lo + width], preferred_element_type=F32)

        qk_pre = proj(o_qk, qkw)
        x3 = jnp.concatenate([prev[None], qk_pre.reshape(ts // SUBLANES, SUBLANES, qkw)], axis=0)
        prev = qk_pre[ts - SUBLANES:, :]
        cur, prv = x3[1:], x3[:-1]
        y = cb_ref[...][None] + cw_ref[QK_CONV - 1:QK_CONV, :][None] * cur
        for j in range(1, QK_CONV):
            shifted = pltpu.roll(jnp.where(sub >= SUBLANES - j, prv, cur), j, axis=1)
            y = y + cw_ref[QK_CONV - 1 - j:QK_CONV - j, :][None] * shifted
        y = y.reshape(ts, qkw)
        act = y * _sigmoid(y)
        q_ref[rows, :] = act[:, :MLSTM_QK_W].astype(BF16)
        kt_ref[:, rows] = (act[:, MLSTM_QK_W:] * (MLSTM_QK_DIM ** -0.5)).T.astype(BF16)

        og_ref[rows, :] = (ng_ref[...] * _sigmoid(proj(o_o, MLSTM_W))).astype(BF16)
        mv_ref[rows, :] = proj(o_v, MLSTM_W).astype(BF16)

        c32 = cos_ref[rows, :]
        s32 = sin_ref[rows, :]
        cos = jnp.concatenate([c32] * 4, axis=-1)
        sin = jnp.concatenate([-s32, s32, -s32, s32], axis=-1)

        def rope(t):
            partner = jnp.where(first_half,
                                pltpu.roll(t, LANES - ATTN_HEAD_DIM // 2, axis=1),
                                pltpu.roll(t, ATTN_HEAD_DIM // 2, axis=1))
            return t * cos + partner * sin

        aq = proj(o_aq, ATTN_W)
        scale = ATTN_HEAD_DIM ** -0.5 * LOG2E
        for c in range(ATTN_W // LANES):
            sl = slice(c * LANES, (c + 1) * LANES)
            aq_ref[rows, sl] = (rope(aq[:, sl]) * scale).astype(BF16)
        akv = proj(o_kv, 2 * ATTN_KV_W)
        akv_ref[rows, :ATTN_KV_W] = rope(akv[:, :ATTN_KV_W]).astype(BF16)
        akv_ref[rows, ATTN_KV_W:] = akv[:, ATTN_KV_W:].astype(BF16)

        gates = proj(o_g, GATE_W) + gb_ref[...]
        gt = gates.T[0:SUBLANES, :]
        comb = jnp.where(row8 < MLSTM_HEADS, gt, _log_sigmoid(gt))
        parts = _split3(comb)
        cums = []
        for c in range(ts // CHUNK):
            cs = slice(c * CHUNK, (c + 1) * CHUNK)
            acc = jnp.zeros((SUBLANES, CHUNK), F32)
            for part in parts:
                acc = acc + jnp.dot(part[:, cs], triu, preferred_element_type=F32)
            cums.append(acc)
        b = jnp.concatenate(cums, axis=1)
        gt_ref[:, rows] = jnp.where(row8 < MLSTM_HEADS, comb - pltpu.roll(b, MLSTM_HEADS, axis=0), b)
    hist_ref[...] = prev


def _inproj(x2, g, w, cos_t, sin_t, cw, cb, gb, ng, seq, l):
    T = x2.shape[0]
    tm = min(TM_IN, seq)
    nseq = seq // tm
    row = lambda i: (i, 0)
    col = lambda i: (0, i)
    pos = lambda i: (i % nseq, 0)
    layer = lambda i: (l, 0, 0)
    return pl.pallas_call(
        functools.partial(_inproj_kernel, tiles_per_seq=nseq, nsub=tm // min(TS_IN, tm)),
        grid=(T // tm,),
        in_specs=[
            pl.BlockSpec((tm, D_MODEL), row),
            pl.BlockSpec((None, 1, D_MODEL), layer),
            pl.BlockSpec((None, D_MODEL, IN_COLS), layer),
            pl.BlockSpec((tm, ATTN_HEAD_DIM // 2), pos),
            pl.BlockSpec((tm, ATTN_HEAD_DIM // 2), pos),
            pl.BlockSpec((None, QK_CONV, 2 * MLSTM_QK_W), layer),
            pl.BlockSpec((None, 1, 2 * MLSTM_QK_W), layer),
            pl.BlockSpec((None, 1, GATE_W), layer),
            pl.BlockSpec((None, 1, MLSTM_W), layer),
        ],
        out_specs=[
            pl.BlockSpec((tm, MLSTM_QK_W), row),
            pl.BlockSpec((MLSTM_QK_W, tm), col),
            pl.BlockSpec((tm, MLSTM_W), row),
            pl.BlockSpec((tm, MLSTM_W), row),
            pl.BlockSpec((SUBLANES, tm), col),
            pl.BlockSpec((tm, ATTN_W), row),
            pl.BlockSpec((tm, 2 * ATTN_KV_W), row),
        ],
        out_shape=[
            jax.ShapeDtypeStruct((T, MLSTM_QK_W), BF16),
            jax.ShapeDtypeStruct((MLSTM_QK_W, T), BF16),
            jax.ShapeDtypeStruct((T, MLSTM_W), BF16),
            jax.ShapeDtypeStruct((T, MLSTM_W), BF16),
            jax.ShapeDtypeStruct((SUBLANES, T), F32),
            jax.ShapeDtypeStruct((T, ATTN_W), BF16),
            jax.ShapeDtypeStruct((T, 2 * ATTN_KV_W), BF16),
        ],
        scratch_shapes=[pltpu.VMEM((SUBLANES, 2 * MLSTM_QK_W), F32)],
        compiler_params=pltpu.CompilerParams(
            dimension_semantics=("arbitrary",), vmem_limit_bytes=VMEM_LIMIT),
        name="inproj",
    )(x2, g, w, cos_t, sin_t, cw, cb, gb, ng)


def _mlstm_kernel(q_ref, kt_ref, v_ref, og_ref, gt_ref, out_ref, c_ref, m_ref, *, nchunk):
    L = CHUNK
    H = MLSTM_HEADS
    DK = MLSTM_QK_DIM
    DV = MLSTM_V_DIM

    @pl.when(pl.program_id(1) == 0)
    def _():
        c_ref[...] = jnp.zeros_like(c_ref)
        m_ref[...] = jnp.zeros_like(m_ref)

    row_i = lax.broadcasted_iota(jnp.int32, (L, L), 0)
    col_i = lax.broadcasted_iota(jnp.int32, (L, L), 1)
    causal = col_i <= row_i
    eye = col_i == row_i
    qlane = lax.broadcasted_iota(jnp.int32, (L, H * DK), 1)
    row8 = lax.broadcasted_iota(jnp.int32, (SUBLANES, LANES), 0)
    ones_blk = jnp.ones((L, DV), BF16)
    neg_inf = jnp.float32(-jnp.inf)
    zero_bf = jnp.zeros((), BF16)

    for c in range(nchunk):
        rows = slice(c * L, (c + 1) * L)
        q = q_ref[rows, :]
        kt = kt_ref[:, rows]
        g8 = gt_ref[:, rows] * LOG2E
        m8 = m_ref[...]

        a_last8 = jnp.maximum(jnp.max(g8, axis=-1, keepdims=True), m8)
        decay8 = jnp.exp2(m8 - a_last8)
        ws8 = jnp.exp2(g8 - a_last8)
        b_last8 = pltpu.roll(g8, H, axis=0)[:, L - 1:L]
        m_ref[...] = jnp.where(row8 < H, b_last8 + a_last8, 0.0)

        qm = jnp.concatenate(
            [jnp.where((qlane >= h * DK) & (qlane < (h + 1) * DK), q, zero_bf) for h in range(H)], axis=0)
        sc = jnp.dot(qm, kt, preferred_element_type=F32)
        qc = jnp.dot(qm, c_ref[...].astype(BF16), preferred_element_type=F32)

        for h in range(H):
            hr = slice(h * L, (h + 1) * L)
            vs = slice(h * DV, (h + 1) * DV)
            ks = slice(h * DK, (h + 1) * DK)
            r_b = jnp.broadcast_to(g8[h:h + 1, :], (L, L))
            b_b = jnp.broadcast_to(g8[H + h:H + h + 1, :], (L, L))
            m_b = jnp.broadcast_to(m8[h:h + 1, :], (L, LANES))

            rmat = jnp.where(causal, r_b, neg_inf)
            a = jnp.maximum(jnp.max(rmat, axis=-1, keepdims=True), m_b)
            w_intra = jnp.exp2(rmat - a)
            w_inter = jnp.exp2(m_b - a)
            b_col = jnp.sum(jnp.where(eye, b_b, 0.0), axis=-1, keepdims=True)
            floor = jnp.exp2(-(b_col + a))

            s = (sc[hr, :] * w_intra).astype(BF16)
            v_ext = jnp.concatenate([v_ref[rows, vs], ones_blk], axis=-1)
            kw_t = (kt[ks, :].astype(F32) * ws8[h:h + 1, :]).astype(BF16)
            res = jnp.dot(jnp.concatenate([s, kw_t], axis=0), v_ext,
                          preferred_element_type=F32)
            num = w_inter * qc[hr, :DV] + res[:L, :DV]
            den = w_inter * qc[hr, DV:] + res[:L, DV:]
            hh = num / jnp.maximum(jnp.abs(den), floor)
            hn = hh * lax.rsqrt(jnp.mean(hh * hh, axis=-1, keepdims=True) + EPS)
            out_ref[rows, vs] = (hn * og_ref[rows, vs].astype(F32)).astype(BF16)

            dec = jnp.broadcast_to(decay8[h:h + 1, :], (DK, LANES))
            c_ref[ks, :] = jnp.concatenate([dec, dec], axis=-1) * c_ref[ks, :] + res[L:, :]


def _mlstm(q, kt, mv, og, gt, batch, seq):
    T = q.shape[0]
    tb = min(TB_MLSTM, seq)
    nb = seq // tb
    row = lambda b, i: (b * nb + i, 0)
    col = lambda b, i: (0, b * nb + i)
    return pl.pallas_call(
        functools.partial(_mlstm_kernel, nchunk=tb // CHUNK),
        grid=(batch, nb),
        in_specs=[
            pl.BlockSpec((tb, MLSTM_QK_W), row),
            pl.BlockSpec((MLSTM_QK_W, tb), col),
            pl.BlockSpec((tb, MLSTM_W), row),
            pl.BlockSpec((tb, MLSTM_W), row),
            pl.BlockSpec((SUBLANES, tb), col),
        ],
        out_specs=pl.BlockSpec((tb, MLSTM_W), row),
        out_shape=jax.ShapeDtypeStruct((T, MLSTM_W), BF16),
        scratch_shapes=[
            pltpu.VMEM((MLSTM_HEADS * MLSTM_QK_DIM, 2 * MLSTM_V_DIM), F32),
            pltpu.VMEM((SUBLANES, LANES), F32),
        ],
        compiler_params=pltpu.CompilerParams(
            dimension_semantics=("arbitrary", "arbitrary"), vmem_limit_bytes=VMEM_LIMIT),
        name="mlstm",
    )(q, kt, mv, og, gt)


def _swa_kernel(sink_ref, q_ref, kv_ref, kvp_ref, out_ref, *, nblk, l):
    W = WINDOW
    G = ATTN_HEADS // ATTN_KV_HEADS
    KW = ATTN_KV_W
    half = LANES // 2
    lane = lax.broadcasted_iota(jnp.int32, (W, LANES), 1)
    left = lane < half
    qpos = lax.broadcasted_iota(jnp.int32, (W, 2 * W), 0)
    kpos = lax.broadcasted_iota(jnp.int32, (W, 2 * W), 1)
    band = (kpos > qpos) & (kpos <= qpos + W)
    not_first = pl.program_id(1) > 0
    neg_inf = jnp.float32(-jnp.inf)
    zero = jnp.zeros((), BF16)

    for j in range(nblk):
        rows = slice(j * W, (j + 1) * W)
        if j == 0:
            kv2 = jnp.concatenate([kvp_ref[...], kv_ref[rows, :]], axis=0)
            valid = band & ((kpos >= W) | not_first)
        else:
            kv2 = kv_ref[(j - 1) * W:(j + 1) * W, :]
            valid = band
        k2 = kv2[:, :KW]
        v2 = kv2[:, KW:]
        qb = q_ref[rows, :]
        qs = jnp.concatenate(
            [jnp.where(left, qb[:, c * LANES:(c + 1) * LANES], zero) for c in range(G)]
            + [jnp.where(left, zero, qb[:, c * LANES:(c + 1) * LANES]) for c in range(G)], axis=0)
        s_all = lax.dot_general(qs, k2, (((1,), (1,)), ((), ())),
                                preferred_element_type=F32)
        ps, invs = [], []
        for h in range(ATTN_HEADS):
            sink = sink_ref[l, h] * LOG2E
            s = jnp.where(valid, s_all[h * W:(h + 1) * W, :], neg_inf)
            mx = jnp.maximum(jnp.max(s, axis=-1, keepdims=True), sink)
            p = jnp.exp2(s - mx)
            denom = jnp.sum(p, axis=-1, keepdims=True) + jnp.exp2(sink - mx)
            ps.append(p.astype(BF16))
            invs.append(1.0 / denom)
        pv = jnp.dot(jnp.concatenate(ps, axis=0), v2, preferred_element_type=F32)
        for c in range(G):
            lo = pv[c * W:(c + 1) * W, :] * invs[c]
            hi = pv[(G + c) * W:(G + c + 1) * W, :] * invs[G + c]
            out_ref[rows, c * LANES:(c + 1) * LANES] = jnp.where(left, lo, hi).astype(BF16)


def _swa(sinks, aq, akv, batch, seq, l):
    T = aq.shape[0]
    tq = min(TQ_SWA, seq)
    nb = seq // tq
    nblk = tq // WINDOW
    nwin = seq // WINDOW
    row = lambda b, i: (b * nb + i, 0)
    prev = lambda b, i: (b * nwin + jnp.maximum(i * nblk - 1, 0), 0)
    return pl.pallas_call(
        functools.partial(_swa_kernel, nblk=nblk, l=l),
        grid=(batch, nb),
        in_specs=[
            pl.BlockSpec(memory_space=pltpu.SMEM),
            pl.BlockSpec((tq, ATTN_W), row),
            pl.BlockSpec((tq, 2 * ATTN_KV_W), row),
            pl.BlockSpec((WINDOW, 2 * ATTN_KV_W), prev),
        ],
        out_specs=pl.BlockSpec((tq, ATTN_W), row),
        out_shape=jax.ShapeDtypeStruct((T, ATTN_W), BF16),
        compiler_params=pltpu.CompilerParams(
            dimension_semantics=("arbitrary", "arbitrary"), vmem_limit_bytes=VMEM_LIMIT),
        name="swa",
    )(sinks, aq, akv, akv)


def _outproj_kernel(m_ref, a_ref, w_ref, x_ref, gpost_ref, gpre_ref, x1_ref, h2_ref, *, nsub):
    ts = x_ref.shape[0] // nsub
    for sb in range(nsub):
        rows = slice(sb * ts, (sb + 1) * ts)
        y = (jnp.dot(m_ref[rows, :], w_ref[:MLSTM_W, :], preferred_element_type=F32)
             + jnp.dot(a_ref[rows, :], w_ref[MLSTM_W:, :], preferred_element_type=F32))
        x1 = x_ref[rows, :] + _rms(y, gpost_ref[...])
        x1_ref[rows, :] = x1
        h2_ref[rows, :] = _rms(x1, gpre_ref[...]).astype(BF16)


def _outproj(m_out, a_out, w, x2, gpost, gpre, l):
    T = x2.shape[0]
    tm = min(TM_OUT, T)
    row = lambda i: (i, 0)
    layer = lambda i: (l, 0, 0)
    return pl.pallas_call(
        functools.partial(_outproj_kernel, nsub=tm // min(TS_OUT, tm)),
        grid=(T // tm,),
        in_specs=[
            pl.BlockSpec((tm, MLSTM_W), row),
            pl.BlockSpec((tm, ATTN_W), row),
            pl.BlockSpec((None, MLSTM_W + ATTN_W, D_MODEL), layer),
            pl.BlockSpec((tm, D_MODEL), row),
            pl.BlockSpec((None, 1, D_MODEL), layer),
            pl.BlockSpec((None, 1, D_MODEL), layer),
        ],
        out_specs=[pl.BlockSpec((tm, D_MODEL), row), pl.BlockSpec((tm, D_MODEL), row)],
        out_shape=[jax.ShapeDtypeStruct((T, D_MODEL), F32), jax.ShapeDtypeStruct((T, D_MODEL), BF16)],
        compiler_params=pltpu.CompilerParams(
            dimension_semantics=("arbitrary",), vmem_limit_bytes=VMEM_LIMIT),
        name="outproj",
    )(m_out, a_out, w, x2, gpost, gpre)


def _ffn_kernel(h_ref, x_ref, wu_ref, wd_ref, cw_ref, gpost_ref, out_ref,
                hs_ref, act_ref, ua_ref, ub_ref, carry_g_ref, carry_v_ref, *, nf, tf, tiles_per_seq):
    tm = h_ref.shape[0]
    hs_ref[...] = h_ref[...]
    seq_start = (pl.program_id(0) % tiles_per_seq) == 0
    sub = lax.broadcasted_iota(jnp.int32, (1, SUBLANES, tf), 1)

    def conv(u, prev, cw):
        u3 = jnp.concatenate([prev[None], u.reshape(u.shape[0] // SUBLANES, SUBLANES, tf)], axis=0)
        cur, prv = u3[1:], u3[:-1]
        s1 = pltpu.roll(jnp.where(sub >= SUBLANES - 1, prv, cur), 1, axis=1)
        s2 = pltpu.roll(jnp.where(sub >= SUBLANES - 2, prv, cur), 2, axis=1)
        y = cw[2:3, :][None] * cur + cw[1:2, :][None] * s1 + cw[0:1, :][None] * s2
        return y.reshape(u.shape[0], tf)

    def gate_cols(f):
        return pl.ds(pl.multiple_of(f * tf, tf), tf)

    def val_cols(f):
        return pl.ds(pl.multiple_of(nf * tf + f * tf, tf), tf)

    tr = TR_FFN
    row_blocks = [slice(r * tr, (r + 1) * tr) for r in range(tm // tr)]

    def up(f, u_ref):
        for rs in row_blocks:
            h = hs_ref[rs, :]
            u_ref[rs, :tf] = jnp.dot(h, wu_ref[:, gate_cols(f)], preferred_element_type=F32)
            u_ref[rs, tf:] = jnp.dot(h, wu_ref[:, val_cols(f)], preferred_element_type=F32)

    def gate_act(f, u_ref):
        pg = jnp.where(seq_start, 0.0, carry_g_ref[f])
        pv = jnp.where(seq_start, 0.0, carry_v_ref[f])
        cwg = cw_ref[:, gate_cols(f)]
        cwv = cw_ref[:, val_cols(f)]
        for rs in row_blocks:
            ug = u_ref[rs, :tf]
            uv = u_ref[rs, tf:]
            gate = conv(ug, pg, cwg)
            val = conv(uv, pv, cwv)
            pg = ug[tr - SUBLANES:, :]
            pv = uv[tr - SUBLANES:, :]
            act_ref[rs, gate_cols(f)] = (gate * _sigmoid(gate) * val).astype(BF16)
        carry_g_ref[f] = pg
        carry_v_ref[f] = pv

    up(0, ua_ref)

    def body(i, carry):
        f = 2 * i
        up(f + 1, ub_ref)
        gate_act(f, ua_ref)
        up(f + 2, ua_ref)
        gate_act(f + 1, ub_ref)
        return carry

    lax.fori_loop(0, (nf - 1) // 2, body, 0)
    gate_act(nf - 1, ua_ref)

    y = jnp.dot(act_ref[...], wd_ref[...], preferred_element_type=F32)
    out_ref[...] = x_ref[...] + _rms(y, gpost_ref[...])


def _ffn(h2, x1, wu, wd, cw, gpost, seq, l):
    T = h2.shape[0]
    tm = min(TM_FFN, seq)
    tf = TF_FFN
    nf = D_FF // tf
    assert nf % 2 == 1 and nf * tf == D_FF
    row = lambda i: (i, 0)
    layer = lambda i: (l, 0, 0)
    return pl.pallas_call(
        functools.partial(_ffn_kernel, nf=nf, tf=tf, tiles_per_seq=seq // tm),
        grid=(T // tm,),
        in_specs=[
            pl.BlockSpec((tm, D_MODEL), row),
            pl.BlockSpec((tm, D_MODEL), row),
            pl.BlockSpec((None, D_MODEL, 2 * D_FF), layer),
            pl.BlockSpec((None, D_FF, D_MODEL), layer),
            pl.BlockSpec((None, FFN_CONV, 2 * D_FF), layer),
            pl.BlockSpec((None, 1, D_MODEL), layer),
        ],
        out_specs=pl.BlockSpec((tm, D_MODEL), row),
        out_shape=jax.ShapeDtypeStruct((T, D_MODEL), F32),
        scratch_shapes=[
            pltpu.VMEM((tm, D_MODEL), BF16),
            pltpu.VMEM((tm, D_FF), BF16),
            pltpu.VMEM((tm, 2 * tf), F32),
            pltpu.VMEM((tm, 2 * tf), F32),
            pltpu.VMEM((nf, SUBLANES, tf), F32),
            pltpu.VMEM((nf, SUBLANES, tf), F32),
        ],
        compiler_params=pltpu.CompilerParams(
            dimension_semantics=("arbitrary",), vmem_limit_bytes=VMEM_LIMIT),
        name="ffn",
    )(h2, x1, wu, wd, cw, gpost)


def _pair_heads(a, axis):
    G = ATTN_HEADS // ATTN_KV_HEADS
    shape = a.shape
    a = a.reshape(shape[:axis] + (ATTN_KV_HEADS, G, ATTN_HEAD_DIM) + shape[axis + 1:])
    a = jnp.swapaxes(a, axis, axis + 1)
    return a.reshape(shape)


def _rope_tables(seq):
    half = ATTN_HEAD_DIM // 2
    reps = LANES // half
    inv = 1.0 / (ROPE_THETA ** (jnp.arange(0, ATTN_HEAD_DIM, 2, dtype=F32) / ATTN_HEAD_DIM))
    pos = jnp.arange(seq, dtype=F32).reshape(seq // reps, reps, 1)
    ang = (pos * inv[None, None, :]).reshape(seq // reps, LANES)
    return jnp.cos(ang).reshape(seq, half), jnp.sin(ang).reshape(seq, half)


def kernel(x, g_pre_mix, w_in, qk_conv_w, qk_conv_b, gate_bias, mh_norm_g, attn_sinks, w_out,
           g_post_mix, g_pre_ffn, w_up, ffn_conv_w, w_down, g_post_ffn):
    batch, seq, _ = x.shape
    depth = w_in.shape[0]
    T = batch * seq
    cos_t, sin_t = _rope_tables(seq)

    n_gate = 2 * MLSTM_HEADS
    o_g = 2 * MLSTM_QK_W + 2 * MLSTM_W
    o_aq = o_g + n_gate
    o_ak = o_aq + ATTN_W
    w_a = jnp.concatenate([
        w_in[:, :, :o_g],
        _pair_heads(w_in[:, :, o_aq:o_ak], 2),
        w_in[:, :, o_ak:],
        w_in[:, :, o_g:o_aq],
        jnp.zeros((depth, D_MODEL, GATE_W - n_gate), w_in.dtype),
    ], axis=2).astype(BF16)
    gb = jnp.concatenate([gate_bias, jnp.zeros((depth, GATE_W - n_gate), F32)], axis=1)[:, None, :]
    w_o = jnp.concatenate([w_out[:, :MLSTM_W], _pair_heads(w_out[:, MLSTM_W:], 1)], axis=1).astype(BF16)
    w_u = w_up.astype(BF16)
    w_d = w_down.astype(BF16)
    vec = lambda p: p[:, None, :]

    x2 = x.reshape(T, D_MODEL)
    for l in range(depth):
        q, kt, mv, og, gt, aq, akv = _inproj(
            x2, vec(g_pre_mix), w_a, cos_t, sin_t, qk_conv_w, vec(qk_conv_b), gb, vec(mh_norm_g), seq, l)
        m_out = _mlstm(q, kt, mv, og, gt, batch, seq)
        a_out = _swa(attn_sinks, aq, akv, batch, seq, l)
        x1, h2 = _outproj(m_out, a_out, w_o, x2, vec(g_post_mix), vec(g_pre_ffn), l)
        x2 = _ffn(h2, x1, w_u, w_d, ffn_conv_w, vec(g_post_ffn), seq, l)
    return x2.reshape(batch, seq, D_MODEL)
```

```python
import functools

import jax
import jax.numpy as jnp
from jax import lax
from jax.experimental import pallas as pl
from jax.experimental.pallas import tpu as pltpu

F32 = jnp.float32
BF16 = jnp.bfloat16

D_MODEL = 1024
MLSTM_HEADS = 4
MLSTM_QK_DIM = 64
MLSTM_V_DIM = 128
MLSTM_QK_W = MLSTM_HEADS * MLSTM_QK_DIM
MLSTM_W = MLSTM_HEADS * MLSTM_V_DIM
CHUNK = 128
QK_CONV = 4
ATTN_HEADS = 8
ATTN_KV_HEADS = 2
ATTN_HEAD_DIM = 64
ATTN_W = ATTN_HEADS * ATTN_HEAD_DIM
ATTN_KV_W = ATTN_KV_HEADS * ATTN_HEAD_DIM
WINDOW = 128
ROPE_THETA = 10000.0
D_FF = 2816
FFN_CONV = 3
EPS = 1e-6
LOG2E = 1.4426950408889634

LANES = 128
SUBLANES = 8
GATE_W = LANES
IN_COLS = 2 * MLSTM_QK_W + 2 * MLSTM_W + ATTN_W + 2 * ATTN_KV_W + GATE_W

TM_MIX, TH_MIX = 1024, 512
TS_OUT = 256
TM_FFN = 1024
TF_FFN = 256
TR_FFN = 1024
VMEM_LIMIT = 56 * 1024 * 1024


def _sigmoid(x):
    return 1.0 / (1.0 + jnp.exp(-x))


def _log_sigmoid(x):
    return jnp.minimum(x, 0.0) - jnp.log1p(jnp.exp(-jnp.abs(x)))


def _rms(x, g):
    return x * lax.rsqrt(jnp.mean(x * x, axis=-1, keepdims=True) + EPS) * g


def _split3(x):
    hi = x.astype(BF16)
    r1 = x - hi.astype(F32)
    mid = r1.astype(BF16)
    lo = (r1 - mid.astype(F32)).astype(BF16)
    return hi, mid, lo


def _inproj_rows(x_ref, g_ref, w_ref, cos_ref, sin_ref, cw_ref, cb_ref, gb_ref, ng_ref,
                 q_ref, kt_ref, mv_ref, og_ref, gt_ref, aq_ref, akv_ref, hist_ref):
    ts = x_ref.shape[0]
    qkw = 2 * MLSTM_QK_W
    o_qk = 0
    o_v = o_qk + qkw
    o_o = o_v + MLSTM_W
    o_aq = o_o + MLSTM_W
    o_kv = o_aq + ATTN_W
    o_g = o_kv + 2 * ATTN_KV_W

    r_i = lax.broadcasted_iota(jnp.int32, (CHUNK, CHUNK), 0)
    c_i = lax.broadcasted_iota(jnp.int32, (CHUNK, CHUNK), 1)
    triu = (r_i <= c_i).astype(BF16)
    row8 = lax.broadcasted_iota(jnp.int32, (SUBLANES, ts), 0)
    sub = lax.broadcasted_iota(jnp.int32, (1, SUBLANES, qkw), 1)
    lane = lax.broadcasted_iota(jnp.int32, (ts, LANES), 1)
    first_half = (lane % ATTN_HEAD_DIM) < (ATTN_HEAD_DIM // 2)

    h = _rms(x_ref[...], g_ref[...]).astype(BF16)

    def proj(lo, width):
        return jnp.dot(h, w_ref[:, lo:lo + width], preferred_element_type=F32)

    qk_pre = proj(o_qk, qkw)
    x3 = jnp.concatenate([hist_ref[...][None], qk_pre.reshape(ts // SUBLANES, SUBLANES, qkw)], axis=0)
    hist_ref[...] = qk_pre[ts - SUBLANES:, :]
    cur, prv = x3[1:], x3[:-1]
    y = cb_ref[...][None] + cw_ref[QK_CONV - 1:QK_CONV, :][None] * cur
    for j in range(1, QK_CONV):
        shifted = pltpu.roll(jnp.where(sub >= SUBLANES - j, prv, cur), j, axis=1)
        y = y + cw_ref[QK_CONV - 1 - j:QK_CONV - j, :][None] * shifted
    y = y.reshape(ts, qkw)
    act = y * _sigmoid(y)
    q_ref[...] = act[:, :MLSTM_QK_W].astype(BF16)
    kt_ref[...] = (act[:, MLSTM_QK_W:] * (MLSTM_QK_DIM ** -0.5)).T.astype(BF16)

    og_ref[...] = (ng_ref[...] * _sigmoid(proj(o_o, MLSTM_W))).astype(BF16)
    mv_ref[...] = proj(o_v, MLSTM_W).astype(BF16)

    c32 = cos_ref[...]
    s32 = sin_ref[...]
    cos = jnp.concatenate([c32] * 4, axis=-1)
    sin = jnp.concatenate([-s32, s32, -s32, s32], axis=-1)

    def rope(t):
        partner = jnp.where(first_half,
                            pltpu.roll(t, LANES - ATTN_HEAD_DIM // 2, axis=1),
                            pltpu.roll(t, ATTN_HEAD_DIM // 2, axis=1))
        return t * cos + partner * sin

    aq = proj(o_aq, ATTN_W)
    scale = ATTN_HEAD_DIM ** -0.5 * LOG2E
    for c in range(ATTN_W // LANES):
        sl = slice(c * LANES, (c + 1) * LANES)
        aq_ref[:, sl] = (rope(aq[:, sl]) * scale).astype(BF16)
    akv = proj(o_kv, 2 * ATTN_KV_W)
    akv_ref[:, :ATTN_KV_W] = rope(akv[:, :ATTN_KV_W]).astype(BF16)
    akv_ref[:, ATTN_KV_W:] = akv[:, ATTN_KV_W:].astype(BF16)

    gates = proj(o_g, GATE_W) + gb_ref[...]
    gt = gates.T[0:SUBLANES, :]
    comb = jnp.where(row8 < MLSTM_HEADS, gt, _log_sigmoid(gt))
    parts = _split3(comb)
    cums = []
    for c in range(ts // CHUNK):
        cs = slice(c * CHUNK, (c + 1) * CHUNK)
        acc = jnp.zeros((SUBLANES, CHUNK), F32)
        for part in parts:
            acc = acc + jnp.dot(part[:, cs], triu, preferred_element_type=F32)
        cums.append(acc)
    b = jnp.concatenate(cums, axis=1)
    gt_ref[...] = jnp.where(row8 < MLSTM_HEADS, comb - pltpu.roll(b, MLSTM_HEADS, axis=0), b)


def _mlstm_rows(q_ref, kt_ref, v_ref, og_ref, gt_ref, out_ref, c_ref, m_ref, nchunk):
    L = CHUNK
    H = MLSTM_HEADS
    DK = MLSTM_QK_DIM
    DV = MLSTM_V_DIM

    row_i = lax.broadcasted_iota(jnp.int32, (L, L), 0)
    col_i = lax.broadcasted_iota(jnp.int32, (L, L), 1)
    causal = col_i <= row_i
    eye = col_i == row_i
    qlane = lax.broadcasted_iota(jnp.int32, (L, H * DK), 1)
    row8 = lax.broadcasted_iota(jnp.int32, (SUBLANES, LANES), 0)
    ones_blk = jnp.ones((L, DV), BF16)
    neg_inf = jnp.float32(-jnp.inf)
    zero_bf = jnp.zeros((), BF16)

    for c in range(nchunk):
        rows = slice(c * L, (c + 1) * L)
        q = q_ref[rows, :]
        kt = kt_ref[:, rows]
        g8 = gt_ref[:, rows] * LOG2E
        m8 = m_ref[...]

        a_last8 = jnp.maximum(jnp.max(g8, axis=-1, keepdims=True), m8)
        decay8 = jnp.exp2(m8 - a_last8)
        ws8 = jnp.exp2(g8 - a_last8)
        b_last8 = pltpu.roll(g8, H, axis=0)[:, L - 1:L]
        m_ref[...] = jnp.where(row8 < H, b_last8 + a_last8, 0.0)

        qm = jnp.concatenate(
            [jnp.where((qlane >= h * DK) & (qlane < (h + 1) * DK), q, zero_bf) for h in range(H)], axis=0)
        sc = jnp.dot(qm, kt, preferred_element_type=F32)
        qc = jnp.dot(qm, c_ref[...].astype(BF16), preferred_element_type=F32)

        for h in range(H):
            hr = slice(h * L, (h + 1) * L)
            vs = slice(h * DV, (h + 1) * DV)
            ks = slice(h * DK, (h + 1) * DK)
            r_b = jnp.broadcast_to(g8[h:h + 1, :], (L, L))
            b_b = jnp.broadcast_to(g8[H + h:H + h + 1, :], (L, L))
            m_b = jnp.broadcast_to(m8[h:h + 1, :], (L, LANES))

            rmat = jnp.where(causal, r_b, neg_inf)
            a = jnp.maximum(jnp.max(rmat, axis=-1, keepdims=True), m_b)
            w_intra = jnp.exp2(rmat - a)
            w_inter = jnp.exp2(m_b - a)
            b_col = jnp.sum(jnp.where(eye, b_b, 0.0), axis=-1, keepdims=True)
            floor = jnp.exp2(-(b_col + a))

            s = (sc[hr, :] * w_intra).astype(BF16)
            v_ext = jnp.concatenate([v_ref[rows, vs], ones_blk], axis=-1)
            kw_t = (kt[ks, :].astype(F32) * ws8[h:h + 1, :]).astype(BF16)
            res = jnp.dot(jnp.concatenate([s, kw_t], axis=0), v_ext,
                          preferred_element_type=F32)
            num = w_inter * qc[hr, :DV] + res[:L, :DV]
            den = w_inter * qc[hr, DV:] + res[:L, DV:]
            hh = num / jnp.maximum(jnp.abs(den), floor)
            hn = hh * lax.rsqrt(jnp.mean(hh * hh, axis=-1, keepdims=True) + EPS)
            out_ref[rows, vs] = (hn * og_ref[rows, vs].astype(F32)).astype(BF16)

            dec = jnp.broadcast_to(decay8[h:h + 1, :], (DK, LANES))
            c_ref[ks, :] = jnp.concatenate([dec, dec], axis=-1) * c_ref[ks, :] + res[L:, :]


def _swa_rows(sink_ref, q_ref, kv_ref, kvp_ref, out_ref, nblk, l, not_first):
    W = WINDOW
    G = ATTN_HEADS // ATTN_KV_HEADS
    KW = ATTN_KV_W
    half = LANES // 2
    lane = lax.broadcasted_iota(jnp.int32, (W, LANES), 1)
    left = lane < half
    qpos = lax.broadcasted_iota(jnp.int32, (W, 2 * W), 0)
    kpos = lax.broadcasted_iota(jnp.int32, (W, 2 * W), 1)
    band = (kpos > qpos) & (kpos <= qpos + W)
    neg_inf = jnp.float32(-jnp.inf)
    zero = jnp.zeros((), BF16)

    for j in range(nblk):
        rows = slice(j * W, (j + 1) * W)
        if j == 0:
            kv2 = jnp.concatenate([kvp_ref[...], kv_ref[rows, :]], axis=0)
            valid = band & ((kpos >= W) | not_first)
        else:
            kv2 = kv_ref[(j - 1) * W:(j + 1) * W, :]
            valid = band
        k2 = kv2[:, :KW]
        v2 = kv2[:, KW:]
        qb = q_ref[rows, :]
        qs = jnp.concatenate(
            [jnp.where(left, qb[:, c * LANES:(c + 1) * LANES], zero) for c in range(G)]
            + [jnp.where(left, zero, qb[:, c * LANES:(c + 1) * LANES]) for c in range(G)], axis=0)
        s_all = lax.dot_general(qs, k2, (((1,), (1,)), ((), ())),
                                preferred_element_type=F32)
        ps, invs = [], []
        for h in range(ATTN_HEADS):
            sink = sink_ref[l, h] * LOG2E
            s = jnp.where(valid, s_all[h * W:(h + 1) * W, :], neg_inf)
            mx = jnp.maximum(jnp.max(s, axis=-1, keepdims=True), sink)
            p = jnp.exp2(s - mx)
            denom = jnp.sum(p, axis=-1, keepdims=True) + jnp.exp2(sink - mx)
            ps.append(p.astype(BF16))
            invs.append(1.0 / denom)
        pv = jnp.dot(jnp.concatenate(ps, axis=0), v2, preferred_element_type=F32)
        for c in range(G):
            lo = pv[c * W:(c + 1) * W, :] * invs[c]
            hi = pv[(G + c) * W:(G + c + 1) * W, :] * invs[G + c]
            out_ref[rows, c * LANES:(c + 1) * LANES] = jnp.where(left, lo, hi).astype(BF16)


def _mixer_kernel(sink_ref, x_ref, g_ref, w_ref, cos_ref, sin_ref, cw_ref, cb_ref, gb_ref, ng_ref,
                  wo_ref, gpost_ref, gpre_ref, x1_ref, h2_ref,
                  q_s, kt_s, v_s, og_s, gt_s, aq_s, akv_s, mo_s, ao_s, hist_ref, c_ref, m_ref, kvp_ref,
                  *, nhalf, nsub, l):
    tm = x_ref.shape[0]
    th = tm // nhalf
    first = pl.program_id(1) == 0

    @pl.when(first)
    def _():
        hist_ref[...] = jnp.zeros_like(hist_ref)
        c_ref[...] = jnp.zeros_like(c_ref)
        m_ref[...] = jnp.zeros_like(m_ref)
        kvp_ref[...] = jnp.zeros_like(kvp_ref)

    for s in range(nhalf):
        rows = pl.ds(s * th, th)
        _inproj_rows(x_ref.at[rows], g_ref, w_ref, cos_ref.at[rows], sin_ref.at[rows], cw_ref, cb_ref, gb_ref,
                     ng_ref, q_s.at[s], kt_s.at[s], v_s.at[s], og_s.at[s], gt_s.at[s], aq_s.at[s], akv_s.at[s],
                     hist_ref)
    for s in range(nhalf):
        rows = pl.ds(s * th, th)
        _mlstm_rows(q_s.at[s], kt_s.at[s], v_s.at[s], og_s.at[s], gt_s.at[s], mo_s.at[rows],
                    c_ref, m_ref, th // CHUNK)
        kvp = kvp_ref if s == 0 else akv_s.at[s - 1, pl.ds(th - WINDOW, WINDOW)]
        _swa_rows(sink_ref, aq_s.at[s], akv_s.at[s], kvp, ao_s.at[rows], th // WINDOW, l,
                  jnp.logical_not(first) if s == 0 else True)
    kvp_ref[...] = akv_s[nhalf - 1, th - WINDOW:, :]

    ts = tm // nsub
    for sb in range(nsub):
        rows = slice(sb * ts, (sb + 1) * ts)
        y = (jnp.dot(mo_s[rows, :], wo_ref[:MLSTM_W, :], preferred_element_type=F32)
             + jnp.dot(ao_s[rows, :], wo_ref[MLSTM_W:, :], preferred_element_type=F32))
        x1 = x_ref[rows, :] + _rms(y, gpost_ref[...])
        x1_ref[rows, :] = x1
        h2_ref[rows, :] = _rms(x1, gpre_ref[...]).astype(BF16)


def _mixer(sinks, x2, g, w, cos_t, sin_t, cw, cb, gb, ng, wo, gpost, gpre, batch, seq, l):
    T = x2.shape[0]
    tm = min(TM_MIX, seq)
    th = min(TH_MIX, tm)
    nb = seq // tm
    nhalf = tm // th
    row = lambda b, i: (b * nb + i, 0)
    pos = lambda b, i: (i, 0)
    layer = lambda b, i: (l, 0, 0)
    return pl.pallas_call(
        functools.partial(_mixer_kernel, nhalf=nhalf, nsub=tm // min(TS_OUT, tm), l=l),
        grid=(batch, nb),
        in_specs=[
            pl.BlockSpec(memory_space=pltpu.SMEM),
            pl.BlockSpec((tm, D_MODEL), row),
            pl.BlockSpec((None, 1, D_MODEL), layer),
            pl.BlockSpec((None, D_MODEL, IN_COLS), layer),
            pl.BlockSpec((tm, ATTN_HEAD_DIM // 2), pos),
            pl.BlockSpec((tm, ATTN_HEAD_DIM // 2), pos),
            pl.BlockSpec((None, QK_CONV, 2 * MLSTM_QK_W), layer),
            pl.BlockSpec((None, 1, 2 * MLSTM_QK_W), layer),
            pl.BlockSpec((None, 1, GATE_W), layer),
            pl.BlockSpec((None, 1, MLSTM_W), layer),
            pl.BlockSpec((None, MLSTM_W + ATTN_W, D_MODEL), layer),
            pl.BlockSpec((None, 1, D_MODEL), layer),
            pl.BlockSpec((None, 1, D_MODEL), layer),
        ],
        out_specs=[pl.BlockSpec((tm, D_MODEL), row), pl.BlockSpec((tm, D_MODEL), row)],
        out_shape=[jax.ShapeDtypeStruct((T, D_MODEL), F32), jax.ShapeDtypeStruct((T, D_MODEL), BF16)],
        scratch_shapes=[
            pltpu.VMEM((nhalf, th, MLSTM_QK_W), BF16),
            pltpu.VMEM((nhalf, MLSTM_QK_W, th), BF16),
            pltpu.VMEM((nhalf, th, MLSTM_W), BF16),
            pltpu.VMEM((nhalf, th, MLSTM_W), BF16),
            pltpu.VMEM((nhalf, SUBLANES, th), F32),
            pltpu.VMEM((nhalf, th, ATTN_W), BF16),
            pltpu.VMEM((nhalf, th, 2 * ATTN_KV_W), BF16),
            pltpu.VMEM((tm, MLSTM_W), BF16),
            pltpu.VMEM((tm, ATTN_W), BF16),
            pltpu.VMEM((SUBLANES, 2 * MLSTM_QK_W), F32),
            pltpu.VMEM((MLSTM_HEADS * MLSTM_QK_DIM, 2 * MLSTM_V_DIM), F32),
            pltpu.VMEM((SUBLANES, LANES), F32),
            pltpu.VMEM((WINDOW, 2 * ATTN_KV_W), BF16),
        ],
        compiler_params=pltpu.CompilerParams(
            dimension_semantics=("arbitrary", "arbitrary"), vmem_limit_bytes=VMEM_LIMIT),
        name="mixer",
    )(sinks, x2, g, w, cos_t, sin_t, cw, cb, gb, ng, wo, gpost, gpre)


def _ffn_kernel(h_ref, x_ref, wu_ref, wd_ref, cw_ref, gpost_ref, out_ref,
                hs_ref, act_ref, ua_ref, ub_ref, carry_g_ref, carry_v_ref, *, nf, tf, tiles_per_seq):
    tm = h_ref.shape[0]
    hs_ref[...] = h_ref[...]
    seq_start = (pl.program_id(0) % tiles_per_seq) == 0
    sub = lax.broadcasted_iota(jnp.int32, (1, SUBLANES, tf), 1)

    def conv(u, prev, cw):
        u3 = jnp.concatenate([prev[None], u.reshape(u.shape[0] // SUBLANES, SUBLANES, tf)], axis=0)
        cur, prv = u3[1:], u3[:-1]
        s1 = pltpu.roll(jnp.where(sub >= SUBLANES - 1, prv, cur), 1, axis=1)
        s2 = pltpu.roll(jnp.where(sub >= SUBLANES - 2, prv, cur), 2, axis=1)
        y = cw[2:3, :][None] * cur + cw[1:2, :][None] * s1 + cw[0:1, :][None] * s2
        return y.reshape(u.shape[0], tf)

    def gate_cols(f):
        return pl.ds(pl.multiple_of(f * tf, tf), tf)

    def val_cols(f):
        return pl.ds(pl.multiple_of(nf * tf + f * tf, tf), tf)

    tr = min(TR_FFN, tm)
    row_blocks = [slice(r * tr, (r + 1) * tr) for r in range(tm // tr)]

    def up(f, u_ref):
        for rs in row_blocks:
            h = hs_ref[rs, :]
            u_ref[rs, :tf] = jnp.dot(h, wu_ref[:, gate_cols(f)], preferred_element_type=F32)
            u_ref[rs, tf:] = jnp.dot(h, wu_ref[:, val_cols(f)], preferred_element_type=F32)

    def gate_act(f, u_ref):
        pg = jnp.where(seq_start, 0.0, carry_g_ref[f])
        pv = jnp.where(seq_start, 0.0, carry_v_ref[f])
        cwg = cw_ref[:, gate_cols(f)]
        cwv = cw_ref[:, val_cols(f)]
        for rs in row_blocks:
            ug = u_ref[rs, :tf]
            uv = u_ref[rs, tf:]
            gate = conv(ug, pg, cwg)
            val = conv(uv, pv, cwv)
            pg = ug[tr - SUBLANES:, :]
            pv = uv[tr - SUBLANES:, :]
            act_ref[rs, gate_cols(f)] = (gate * _sigmoid(gate) * val).astype(BF16)
        carry_g_ref[f] = pg
        carry_v_ref[f] = pv

    up(0, ua_ref)

    def body(i, carry):
        f = 2 * i
        up(f + 1, ub_ref)
        gate_act(f, ua_ref)
        up(f + 2, ua_ref)
        gate_act(f + 1, ub_ref)
        return carry

    lax.fori_loop(0, (nf - 1) // 2, body, 0)
    gate_act(nf - 1, ua_ref)

    y = jnp.dot(act_ref[...], wd_ref[...], preferred_element_type=F32)
    out_ref[...] = x_ref[...] + _rms(y, gpost_ref[...])


def _ffn(h2, x1, wu, wd, cw, gpost, seq, l):
    T = h2.shape[0]
    tm = min(TM_FFN, seq)
    tf = TF_FFN
    nf = D_FF // tf
    assert nf % 2 == 1 and nf * tf == D_FF
    row = lambda i: (i, 0)
    layer = lambda i: (l, 0, 0)
    return pl.pallas_call(
        functools.partial(_ffn_kernel, nf=nf, tf=tf, tiles_per_seq=seq // tm),
        grid=(T // tm,),
        in_specs=[
            pl.BlockSpec((tm, D_MODEL), row),
            pl.BlockSpec((tm, D_MODEL), row),
            pl.BlockSpec((None, D_MODEL, 2 * D_FF), layer),
            pl.BlockSpec((None, D_FF, D_MODEL), layer),
            pl.BlockSpec((None, FFN_CONV, 2 * D_FF), layer),
            pl.BlockSpec((None, 1, D_MODEL), layer),
        ],
        out_specs=pl.BlockSpec((tm, D_MODEL), row),
        out_shape=jax.ShapeDtypeStruct((T, D_MODEL), F32),
        scratch_shapes=[
            pltpu.VMEM((tm, D_MODEL), BF16),
            pltpu.VMEM((tm, D_FF), BF16),
            pltpu.VMEM((tm, 2 * tf), F32),
            pltpu.VMEM((tm, 2 * tf), F32),
            pltpu.VMEM((nf, SUBLANES, tf), F32),
            pltpu.VMEM((nf, SUBLANES, tf), F32),
        ],
        compiler_params=pltpu.CompilerParams(
            dimension_semantics=("arbitrary",), vmem_limit_bytes=VMEM_LIMIT),
        name="ffn",
    )(h2, x1, wu, wd, cw, gpost)


def _pair_heads(a, axis):
    G = ATTN_HEADS // ATTN_KV_HEADS
    shape = a.shape
    a = a.reshape(shape[:axis] + (ATTN_KV_HEADS, G, ATTN_HEAD_DIM) + shape[axis + 1:])
    a = jnp.swapaxes(a, axis, axis + 1)
    return a.reshape(shape)


def _rope_tables(seq):
    half = ATTN_HEAD_DIM // 2
    reps = LANES // half
    inv = 1.0 / (ROPE_THETA ** (jnp.arange(0, ATTN_HEAD_DIM, 2, dtype=F32) / ATTN_HEAD_DIM))
    pos = jnp.arange(seq, dtype=F32).reshape(seq // reps, reps, 1)
    ang = (pos * inv[None, None, :]).reshape(seq // reps, LANES)
    cos, sin = lax.optimization_barrier((jnp.cos(ang), jnp.sin(ang)))
    return cos.reshape(seq, half), sin.reshape(seq, half)


def kernel(x, g_pre_mix, w_in, qk_conv_w, qk_conv_b, gate_bias, mh_norm_g, attn_sinks, w_out,
           g_post_mix, g_pre_ffn, w_up, ffn_conv_w, w_down, g_post_ffn):
    batch, seq, _ = x.shape
    depth = w_in.shape[0]
    T = batch * seq
    cos_t, sin_t = _rope_tables(seq)

    n_gate = 2 * MLSTM_HEADS
    o_g = 2 * MLSTM_QK_W + 2 * MLSTM_W
    o_aq = o_g + n_gate
    o_ak = o_aq + ATTN_W
    w_a = jnp.concatenate([
        w_in[:, :, :o_g],
        _pair_heads(w_in[:, :, o_aq:o_ak], 2),
        w_in[:, :, o_ak:],
        w_in[:, :, o_g:o_aq],
        jnp.zeros((depth, D_MODEL, GATE_W - n_gate), w_in.dtype),
    ], axis=2).astype(BF16)
    gb = jnp.concatenate([gate_bias, jnp.zeros((depth, GATE_W - n_gate), F32)], axis=1)[:, None, :]
    w_o = jnp.concatenate([w_out[:, :MLSTM_W], _pair_heads(w_out[:, MLSTM_W:], 1)], axis=1).astype(BF16)
    w_u = w_up.astype(BF16)
    w_d = w_down.astype(BF16)
    vec = lambda p: p[:, None, :]

    x2 = x.reshape(T, D_MODEL)
    for l in range(depth):
        x1, h2 = _mixer(attn_sinks, x2, vec(g_pre_mix), w_a, cos_t, sin_t, qk_conv_w, vec(qk_conv_b), gb,
                        vec(mh_norm_g), w_o, vec(g_post_mix), vec(g_pre_ffn), batch, seq, l)
        x2 = _ffn(h2, x1, w_u, w_d, ffn_conv_w, vec(g_post_ffn), seq, l)
    return x2.reshape(batch, seq, D_MODEL)
```

```python
import functools

import jax
import jax.numpy as jnp
from jax import lax
from jax.experimental import pallas as pl
from jax.experimental.pallas import tpu as pltpu

F32 = jnp.float32
BF16 = jnp.bfloat16

D_MODEL = 1024
MLSTM_HEADS = 4
MLSTM_QK_DIM = 64
MLSTM_V_DIM = 128
MLSTM_QK_W = MLSTM_HEADS * MLSTM_QK_DIM
MLSTM_W = MLSTM_HEADS * MLSTM_V_DIM
CHUNK = 128
QK_CONV = 4
ATTN_HEADS = 8
ATTN_KV_HEADS = 2
ATTN_HEAD_DIM = 64
ATTN_W = ATTN_HEADS * ATTN_HEAD_DIM
ATTN_KV_W = ATTN_KV_HEADS * ATTN_HEAD_DIM
WINDOW = 128
ROPE_THETA = 10000.0
D_FF = 2816
FFN_CONV = 3
EPS = 1e-6
LOG2E = 1.4426950408889634

LANES = 128
SUBLANES = 8
GATE_W = LANES
IN_COLS = 2 * MLSTM_QK_W + 2 * MLSTM_W + ATTN_W + 2 * ATTN_KV_W + GATE_W

TM_MIX, TH_MIX = 1024, 512
TS_OUT = 256
TM_FFN = 1024
TF_FFN = 256
TR_FFN = 1024
VMEM_LIMIT = 56 * 1024 * 1024


def _sigmoid(x):
    return 1.0 / (1.0 + jnp.exp(-x))


def _log_sigmoid(x):
    return jnp.minimum(x, 0.0) - jnp.log1p(jnp.exp(-jnp.abs(x)))


def _rms(x, g):
    return x * lax.rsqrt(jnp.mean(x * x, axis=-1, keepdims=True) + EPS) * g


def _split3(x):
    hi = x.astype(BF16)
    r1 = x - hi.astype(F32)
    mid = r1.astype(BF16)
    lo = (r1 - mid.astype(F32)).astype(BF16)
    return hi, mid, lo


def _inproj_rows(x_ref, g_ref, w_ref, cos_ref, sin_ref, cw_ref, cb_ref, gb_ref, ng_ref,
                 q_ref, kt_ref, mv_ref, og_ref, gt_ref, aq_ref, akv_ref, hist_ref):
    ts = x_ref.shape[0]
    qkw = 2 * MLSTM_QK_W
    o_qk = 0
    o_v = o_qk + qkw
    o_o = o_v + MLSTM_W
    o_aq = o_o + MLSTM_W
    o_kv = o_aq + ATTN_W
    o_g = o_kv + 2 * ATTN_KV_W

    r_i = lax.broadcasted_iota(jnp.int32, (CHUNK, CHUNK), 0)
    c_i = lax.broadcasted_iota(jnp.int32, (CHUNK, CHUNK), 1)
    triu = (r_i <= c_i).astype(BF16)
    row8 = lax.broadcasted_iota(jnp.int32, (SUBLANES, ts), 0)
    sub = lax.broadcasted_iota(jnp.int32, (1, SUBLANES, qkw), 1)
    lane = lax.broadcasted_iota(jnp.int32, (ts, LANES), 1)
    first_half = (lane % ATTN_HEAD_DIM) < (ATTN_HEAD_DIM // 2)

    h = _rms(x_ref[...], g_ref[...]).astype(BF16)

    def proj(lo, width):
        return jnp.dot(h, w_ref[:, lo:lo + width], preferred_element_type=F32)

    qk_pre = proj(o_qk, qkw)
    x3 = jnp.concatenate([hist_ref[...][None], qk_pre.reshape(ts // SUBLANES, SUBLANES, qkw)], axis=0)
    hist_ref[...] = qk_pre[ts - SUBLANES:, :]
    cur, prv = x3[1:], x3[:-1]
    y = cb_ref[...][None] + cw_ref[QK_CONV - 1:QK_CONV, :][None] * cur
    for j in range(1, QK_CONV):
        shifted = pltpu.roll(jnp.where(sub >= SUBLANES - j, prv, cur), j, axis=1)
        y = y + cw_ref[QK_CONV - 1 - j:QK_CONV - j, :][None] * shifted
    y = y.reshape(ts, qkw)
    act = y * _sigmoid(y)
    q_ref[...] = act[:, :MLSTM_QK_W].astype(BF16)
    kt_ref[...] = (act[:, MLSTM_QK_W:] * (MLSTM_QK_DIM ** -0.5)).T.astype(BF16)

    og_ref[...] = (ng_ref[...] * _sigmoid(proj(o_o, MLSTM_W))).astype(BF16)
    mv_ref[...] = proj(o_v, MLSTM_W).astype(BF16)

    c32 = cos_ref[...]
    s32 = sin_ref[...]
    cos = jnp.concatenate([c32] * 4, axis=-1)
    sin = jnp.concatenate([-s32, s32, -s32, s32], axis=-1)

    def rope(t):
        partner = jnp.where(first_half,
                            pltpu.roll(t, LANES - ATTN_HEAD_DIM // 2, axis=1),
                            pltpu.roll(t, ATTN_HEAD_DIM // 2, axis=1))
        return t * cos + partner * sin

    aq = proj(o_aq, ATTN_W)
    scale = ATTN_HEAD_DIM ** -0.5 * LOG2E
    for c in range(ATTN_W // LANES):
        sl = slice(c * LANES, (c + 1) * LANES)
        aq_ref[:, sl] = (rope(aq[:, sl]) * scale).astype(BF16)
    akv = proj(o_kv, 2 * ATTN_KV_W)
    akv_ref[:, :ATTN_KV_W] = rope(akv[:, :ATTN_KV_W]).astype(BF16)
    akv_ref[:, ATTN_KV_W:] = akv[:, ATTN_KV_W:].astype(BF16)

    gates = proj(o_g, GATE_W) + gb_ref[...]
    gt = gates.T[0:SUBLANES, :]
    comb = jnp.where(row8 < MLSTM_HEADS, gt, _log_sigmoid(gt))
    parts = _split3(comb)
    cums = []
    for c in range(ts // CHUNK):
        cs = slice(c * CHUNK, (c + 1) * CHUNK)
        acc = jnp.zeros((SUBLANES, CHUNK), F32)
        for part in parts:
            acc = acc + jnp.dot(part[:, cs], triu, preferred_element_type=F32)
        cums.append(acc)
    b = jnp.concatenate(cums, axis=1)
    gt_ref[...] = jnp.where(row8 < MLSTM_HEADS, comb - pltpu.roll(b, MLSTM_HEADS, axis=0), b)


def _mlstm_rows(q_ref, kt_ref, v_ref, og_ref, gt_ref, out_ref, c_ref, m_ref, nchunk):
    L = CHUNK
    H = MLSTM_HEADS
    DK = MLSTM_QK_DIM
    DV = MLSTM_V_DIM

    row_i = lax.broadcasted_iota(jnp.int32, (L, L), 0)
    col_i = lax.broadcasted_iota(jnp.int32, (L, L), 1)
    causal = col_i <= row_i
    eye = col_i == row_i
    qlane = lax.broadcasted_iota(jnp.int32, (L, H * DK), 1)
    row8 = lax.broadcasted_iota(jnp.int32, (SUBLANES, LANES), 0)
    ones_blk = jnp.ones((L, DV), BF16)
    neg_inf = jnp.float32(-jnp.inf)
    zero_bf = jnp.zeros((), BF16)

    for c in range(nchunk):
        rows = slice(c * L, (c + 1) * L)
        q = q_ref[rows, :]
        kt = kt_ref[:, rows]
        g8 = gt_ref[:, rows] * LOG2E
        m8 = m_ref[...]

        a_last8 = jnp.maximum(jnp.max(g8, axis=-1, keepdims=True), m8)
        decay8 = jnp.exp2(m8 - a_last8)
        ws8 = jnp.exp2(g8 - a_last8)
        b_last8 = pltpu.roll(g8, H, axis=0)[:, L - 1:L]
        m_ref[...] = jnp.where(row8 < H, b_last8 + a_last8, 0.0)

        qm = jnp.concatenate(
            [jnp.where((qlane >= h * DK) & (qlane < (h + 1) * DK), q, zero_bf) for h in range(H)], axis=0)
        sc = jnp.dot(qm, kt, preferred_element_type=F32)
        qc = jnp.dot(qm, c_ref[...].astype(BF16), preferred_element_type=F32)

        for h in range(H):
            hr = slice(h * L, (h + 1) * L)
            vs = slice(h * DV, (h + 1) * DV)
            ks = slice(h * DK, (h + 1) * DK)
            r_b = jnp.broadcast_to(g8[h:h + 1, :], (L, L))
            b_b = jnp.broadcast_to(g8[H + h:H + h + 1, :], (L, L))
            m_b = jnp.broadcast_to(m8[h:h + 1, :], (L, LANES))

            rmat = jnp.where(causal, r_b, neg_inf)
            a = jnp.maximum(jnp.max(rmat, axis=-1, keepdims=True), m_b)
            w_intra = jnp.exp2(rmat - a)
            w_inter = jnp.exp2(m_b - a)
            b_col = jnp.sum(jnp.where(eye, b_b, 0.0), axis=-1, keepdims=True)
            floor = jnp.exp2(-(b_col + a))

            s = (sc[hr, :] * w_intra).astype(BF16)
            v_ext = jnp.concatenate([v_ref[rows, vs], ones_blk], axis=-1)
            kw_t = (kt[ks, :].astype(F32) * ws8[h:h + 1, :]).astype(BF16)
            res = jnp.dot(jnp.concatenate([s, kw_t], axis=0), v_ext,
                          preferred_element_type=F32)
            num = w_inter * qc[hr, :DV] + res[:L, :DV]
            den = w_inter * qc[hr, DV:] + res[:L, DV:]
            hh = num / jnp.maximum(jnp.abs(den), floor)
            hn = hh * lax.rsqrt(jnp.mean(hh * hh, axis=-1, keepdims=True) + EPS)
            out_ref[rows, vs] = (hn * og_ref[rows, vs].astype(F32)).astype(BF16)

            dec = jnp.broadcast_to(decay8[h:h + 1, :], (DK, LANES))
            c_ref[ks, :] = jnp.concatenate([dec, dec], axis=-1) * c_ref[ks, :] + res[L:, :]


def _swa_rows(sink_ref, q_ref, kv_ref, kvp_ref, out_ref, nblk, l, not_first):
    W = WINDOW
    G = ATTN_HEADS // ATTN_KV_HEADS
    KW = ATTN_KV_W
    half = LANES // 2
    lane = lax.broadcasted_iota(jnp.int32, (W, LANES), 1)
    left = lane < half
    qpos = lax.broadcasted_iota(jnp.int32, (W, 2 * W), 0)
    kpos = lax.broadcasted_iota(jnp.int32, (W, 2 * W), 1)
    band = (kpos > qpos) & (kpos <= qpos + W)
    neg_inf = jnp.float32(-jnp.inf)
    zero = jnp.zeros((), BF16)

    for j in range(nblk):
        rows = slice(j * W, (j + 1) * W)
        if j == 0:
            kv2 = jnp.concatenate([kvp_ref[...], kv_ref[rows, :]], axis=0)
            valid = band & ((kpos >= W) | not_first)
        else:
            kv2 = kv_ref[(j - 1) * W:(j + 1) * W, :]
            valid = band
        k2 = kv2[:, :KW]
        v2 = kv2[:, KW:]
        qb = q_ref[rows, :]
        qs = jnp.concatenate(
            [jnp.where(left, qb[:, c * LANES:(c + 1) * LANES], zero) for c in range(G)]
            + [jnp.where(left, zero, qb[:, c * LANES:(c + 1) * LANES]) for c in range(G)], axis=0)
        s_all = lax.dot_general(qs, k2, (((1,), (1,)), ((), ())),
                                preferred_element_type=F32)
        ps, invs = [], []
        for h in range(ATTN_HEADS):
            sink = sink_ref[l, h] * LOG2E
            s = jnp.where(valid, s_all[h * W:(h + 1) * W, :], neg_inf)
            mx = jnp.maximum(jnp.max(s, axis=-1, keepdims=True), sink)
            p = jnp.exp2(s - mx)
            denom = jnp.sum(p, axis=-1, keepdims=True) + jnp.exp2(sink - mx)
            ps.append(p.astype(BF16))
            invs.append(1.0 / denom)
        pv = jnp.dot(jnp.concatenate(ps, axis=0), v2, preferred_element_type=F32)
        for c in range(G):
            lo = pv[c * W:(c + 1) * W, :] * invs[c]
            hi = pv[(G + c) * W:(G + c + 1) * W, :] * invs[G + c]
            out_ref[rows, c * LANES:(c + 1) * LANES] = jnp.where(left, lo, hi).astype(BF16)


def _mixer_kernel(sink_ref, x_ref, g_ref, w_ref, cos_ref, sin_ref, cw_ref, cb_ref, gb_ref, ng_ref,
                  wo_ref, gpost_ref, gpre_ref, x1_ref, h2_ref,
                  q_s, kt_s, v_s, og_s, gt_s, aq_s, akv_s, mo_s, ao_s, hist_ref, c_ref, m_ref, kvp_ref,
                  *, nhalf, nsub, l):
    tm = x_ref.shape[0]
    th = tm // nhalf
    first = pl.program_id(1) == 0

    @pl.when(first)
    def _():
        hist_ref[...] = jnp.zeros_like(hist_ref)
        c_ref[...] = jnp.zeros_like(c_ref)
        m_ref[...] = jnp.zeros_like(m_ref)
        kvp_ref[...] = jnp.zeros_like(kvp_ref)

    for s in range(nhalf):
        rows = pl.ds(s * th, th)
        _inproj_rows(x_ref.at[rows], g_ref, w_ref, cos_ref.at[rows], sin_ref.at[rows], cw_ref, cb_ref, gb_ref,
                     ng_ref, q_s.at[s], kt_s.at[s], v_s.at[s], og_s.at[s], gt_s.at[s], aq_s.at[s], akv_s.at[s],
                     hist_ref)
    for s in range(nhalf):
        rows = pl.ds(s * th, th)
        _mlstm_rows(q_s.at[s], kt_s.at[s], v_s.at[s], og_s.at[s], gt_s.at[s], mo_s.at[rows],
                    c_ref, m_ref, th // CHUNK)
        kvp = kvp_ref if s == 0 else akv_s.at[s - 1, pl.ds(th - WINDOW, WINDOW)]
        _swa_rows(sink_ref, aq_s.at[s], akv_s.at[s], kvp, ao_s.at[rows], th // WINDOW, l,
                  jnp.logical_not(first) if s == 0 else True)
    kvp_ref[...] = akv_s[nhalf - 1, th - WINDOW:, :]

    ts = tm // nsub
    for sb in range(nsub):
        rows = slice(sb * ts, (sb + 1) * ts)
        y = (jnp.dot(mo_s[rows, :], wo_ref[:MLSTM_W, :], preferred_element_type=F32)
             + jnp.dot(ao_s[rows, :], wo_ref[MLSTM_W:, :], preferred_element_type=F32))
        x1 = x_ref[rows, :] + _rms(y, gpost_ref[...])
        x1_ref[rows, :] = x1
        h2_ref[rows, :] = _rms(x1, gpre_ref[...]).astype(BF16)


def _mixer(sinks, x2, g, w, cos_t, sin_t, cw, cb, gb, ng, wo, gpost, gpre, batch, seq, l):
    T = x2.shape[0]
    tm = min(TM_MIX, seq)
    th = min(TH_MIX, tm)
    nb = seq // tm
    nhalf = tm // th
    row = lambda b, i: (b * nb + i, 0)
    pos = lambda b, i: (i, 0)
    layer = lambda b, i: (l, 0, 0)
    return pl.pallas_call(
        functools.partial(_mixer_kernel, nhalf=nhalf, nsub=tm // min(TS_OUT, tm), l=l),
        grid=(batch, nb),
        in_specs=[
            pl.BlockSpec(memory_space=pltpu.SMEM),
            pl.BlockSpec((tm, D_MODEL), row),
            pl.BlockSpec((None, 1, D_MODEL), layer),
            pl.BlockSpec((None, D_MODEL, IN_COLS), layer),
            pl.BlockSpec((tm, ATTN_HEAD_DIM // 2), pos),
            pl.BlockSpec((tm, ATTN_HEAD_DIM // 2), pos),
            pl.BlockSpec((None, QK_CONV, 2 * MLSTM_QK_W), layer),
            pl.BlockSpec((None, 1, 2 * MLSTM_QK_W), layer),
            pl.BlockSpec((None, 1, GATE_W), layer),
            pl.BlockSpec((None, 1, MLSTM_W), layer),
            pl.BlockSpec((None, MLSTM_W + ATTN_W, D_MODEL), layer),
            pl.BlockSpec((None, 1, D_MODEL), layer),
            pl.BlockSpec((None, 1, D_MODEL), layer),
        ],
        out_specs=[pl.BlockSpec((tm, D_MODEL), row), pl.BlockSpec((tm, D_MODEL), row)],
        out_shape=[jax.ShapeDtypeStruct((T, D_MODEL), F32), jax.ShapeDtypeStruct((T, D_MODEL), BF16)],
        scratch_shapes=[
            pltpu.VMEM((nhalf, th, MLSTM_QK_W), BF16),
            pltpu.VMEM((nhalf, MLSTM_QK_W, th), BF16),
            pltpu.VMEM((nhalf, th, MLSTM_W), BF16),
            pltpu.VMEM((nhalf, th, MLSTM_W), BF16),
            pltpu.VMEM((nhalf, SUBLANES, th), F32),
            pltpu.VMEM((nhalf, th, ATTN_W), BF16),
            pltpu.VMEM((nhalf, th, 2 * ATTN_KV_W), BF16),
            pltpu.VMEM((tm, MLSTM_W), BF16),
            pltpu.VMEM((tm, ATTN_W), BF16),
            pltpu.VMEM((SUBLANES, 2 * MLSTM_QK_W), F32),
            pltpu.VMEM((MLSTM_HEADS * MLSTM_QK_DIM, 2 * MLSTM_V_DIM), F32),
            pltpu.VMEM((SUBLANES, LANES), F32),
            pltpu.VMEM((WINDOW, 2 * ATTN_KV_W), BF16),
        ],
        compiler_params=pltpu.CompilerParams(
            dimension_semantics=("arbitrary", "arbitrary"), vmem_limit_bytes=VMEM_LIMIT),
        name="mixer",
    )(sinks, x2, g, w, cos_t, sin_t, cw, cb, gb, ng, wo, gpost, gpre)


def _ffn_kernel(h_ref, x_ref, wu_ref, wd_ref, cw_ref, gpost_ref, out_ref,
                hs_ref, act_ref, ua_ref, ub_ref, carry_g_ref, carry_v_ref, *, nf, tf, tiles_per_seq):
    tm = h_ref.shape[0]
    hs_ref[...] = h_ref[...]
    seq_start = (pl.program_id(0) % tiles_per_seq) == 0
    sub = lax.broadcasted_iota(jnp.int32, (1, SUBLANES, tf), 1)

    def conv(u, prev, cw):
        u3 = jnp.concatenate([prev[None], u.reshape(u.shape[0] // SUBLANES, SUBLANES, tf)], axis=0)
        cur, prv = u3[1:], u3[:-1]
        s1 = pltpu.roll(jnp.where(sub >= SUBLANES - 1, prv, cur), 1, axis=1)
        s2 = pltpu.roll(jnp.where(sub >= SUBLANES - 2, prv, cur), 2, axis=1)
        y = cw[2:3, :][None] * cur + cw[1:2, :][None] * s1 + cw[0:1, :][None] * s2
        return y.reshape(u.shape[0], tf)

    def gate_cols(f):
        return pl.ds(pl.multiple_of(f * tf, tf), tf)

    def val_cols(f):
        return pl.ds(pl.multiple_of(nf * tf + f * tf, tf), tf)

    tr = min(TR_FFN, tm)
    row_blocks = [slice(r * tr, (r + 1) * tr) for r in range(tm // tr)]

    def up(f, u_ref):
        for rs in row_blocks:
            h = hs_ref[rs, :]
            u_ref[rs, :tf] = jnp.dot(h, wu_ref[:, gate_cols(f)], preferred_element_type=F32)
            u_ref[rs, tf:] = jnp.dot(h, wu_ref[:, val_cols(f)], preferred_element_type=F32)

    def gate_act(f, u_ref):
        pg = jnp.where(seq_start, 0.0, carry_g_ref[f])
        pv = jnp.where(seq_start, 0.0, carry_v_ref[f])
        cwg = cw_ref[:, gate_cols(f)]
        cwv = cw_ref[:, val_cols(f)]
        for rs in row_blocks:
            ug = u_ref[rs, :tf]
            uv = u_ref[rs, tf:]
            gate = conv(ug, pg, cwg)
            val = conv(uv, pv, cwv)
            pg = ug[tr - SUBLANES:, :]
            pv = uv[tr - SUBLANES:, :]
            act_ref[rs, gate_cols(f)] = (gate * _sigmoid(gate) * val).astype(BF16)
        carry_g_ref[f] = pg
        carry_v_ref[f] = pv

    up(0, ua_ref)

    def body(i, carry):
        f = 2 * i
        up(f + 1, ub_ref)
        gate_act(f, ua_ref)
        up(f + 2, ua_ref)
        gate_act(f + 1, ub_ref)
        return carry

    lax.fori_loop(0, (nf - 1) // 2, body, 0, unroll=True)
    gate_act(nf - 1, ua_ref)

    y = jnp.dot(act_ref[...], wd_ref[...], preferred_element_type=F32)
    out_ref[...] = x_ref[...] + _rms(y, gpost_ref[...])


def _ffn(h2, x1, wu, wd, cw, gpost, seq, l):
    T = h2.shape[0]
    tm = min(TM_FFN, seq)
    tf = TF_FFN
    nf = D_FF // tf
    assert nf % 2 == 1 and nf * tf == D_FF
    row = lambda i: (i, 0)
    layer = lambda i: (l, 0, 0)
    return pl.pallas_call(
        functools.partial(_ffn_kernel, nf=nf, tf=tf, tiles_per_seq=seq // tm),
        grid=(T // tm,),
        in_specs=[
            pl.BlockSpec((tm, D_MODEL), row),
            pl.BlockSpec((tm, D_MODEL), row),
            pl.BlockSpec((None, D_MODEL, 2 * D_FF), layer),
            pl.BlockSpec((None, D_FF, D_MODEL), layer),
            pl.BlockSpec((None, FFN_CONV, 2 * D_FF), layer),
            pl.BlockSpec((None, 1, D_MODEL), layer),
        ],
        out_specs=pl.BlockSpec((tm, D_MODEL), row),
        out_shape=jax.ShapeDtypeStruct((T, D_MODEL), F32),
        scratch_shapes=[
            pltpu.VMEM((tm, D_MODEL), BF16),
            pltpu.VMEM((tm, D_FF), BF16),
            pltpu.VMEM((tm, 2 * tf), F32),
            pltpu.VMEM((tm, 2 * tf), F32),
            pltpu.VMEM((nf, SUBLANES, tf), F32),
            pltpu.VMEM((nf, SUBLANES, tf), F32),
        ],
        compiler_params=pltpu.CompilerParams(
            dimension_semantics=("arbitrary",), vmem_limit_bytes=VMEM_LIMIT),
        name="ffn",
    )(h2, x1, wu, wd, cw, gpost)


def _pair_heads(a, axis):
    G = ATTN_HEADS // ATTN_KV_HEADS
    shape = a.shape
    a = a.reshape(shape[:axis] + (ATTN_KV_HEADS, G, ATTN_HEAD_DIM) + shape[axis + 1:])
    a = jnp.swapaxes(a, axis, axis + 1)
    return a.reshape(shape)


def _rope_tables(seq):
    half = ATTN_HEAD_DIM // 2
    reps = LANES // half
    inv = 1.0 / (ROPE_THETA ** (jnp.arange(0, ATTN_HEAD_DIM, 2, dtype=F32) / ATTN_HEAD_DIM))
    pos = jnp.arange(seq, dtype=F32).reshape(seq // reps, reps, 1)
    ang = (pos * inv[None, None, :]).reshape(seq // reps, LANES)
    cos, sin = lax.optimization_barrier((jnp.cos(ang), jnp.sin(ang)))
    return cos.reshape(seq, half), sin.reshape(seq, half)


def kernel(x, g_pre_mix, w_in, qk_conv_w, qk_conv_b, gate_bias, mh_norm_g, attn_sinks, w_out,
           g_post_mix, g_pre_ffn, w_up, ffn_conv_w, w_down, g_post_ffn):
    batch, seq, _ = x.shape
    depth = w_in.shape[0]
    T = batch * seq
    cos_t, sin_t = _rope_tables(seq)

    n_gate = 2 * MLSTM_HEADS
    o_g = 2 * MLSTM_QK_W + 2 * MLSTM_W
    o_aq = o_g + n_gate
    o_ak = o_aq + ATTN_W
    w_a = jnp.concatenate([
        w_in[:, :, :o_g],
        _pair_heads(w_in[:, :, o_aq:o_ak], 2),
        w_in[:, :, o_ak:],
        w_in[:, :, o_g:o_aq],
        jnp.zeros((depth, D_MODEL, GATE_W - n_gate), w_in.dtype),
    ], axis=2).astype(BF16)
    gb = jnp.concatenate([gate_bias, jnp.zeros((depth, GATE_W - n_gate), F32)], axis=1)[:, None, :]
    w_o = jnp.concatenate([w_out[:, :MLSTM_W], _pair_heads(w_out[:, MLSTM_W:], 1)], axis=1).astype(BF16)
    w_u = w_up.astype(BF16)
    w_d = w_down.astype(BF16)
    vec = lambda p: p[:, None, :]

    x2 = x.reshape(T, D_MODEL)
    for l in range(depth):
        x1, h2 = _mixer(attn_sinks, x2, vec(g_pre_mix), w_a, cos_t, sin_t, qk_conv_w, vec(qk_conv_b), gb,
                        vec(mh_norm_g), w_o, vec(g_post_mix), vec(g_pre_ffn), batch, seq, l)
        x2 = _ffn(h2, x1, w_u, w_d, ffn_conv_w, vec(g_post_ffn), seq, l)
    return x2.reshape(batch, seq, D_MODEL)
```

```python
import functools

import jax
import jax.numpy as jnp
from jax import lax
from jax.experimental import pallas as pl
from jax.experimental.pallas import tpu as pltpu

F32 = jnp.float32
BF16 = jnp.bfloat16

D_MODEL = 1024
MLSTM_HEADS = 4
MLSTM_QK_DIM = 64
MLSTM_V_DIM = 128
MLSTM_QK_W = MLSTM_HEADS * MLSTM_QK_DIM
MLSTM_W = MLSTM_HEADS * MLSTM_V_DIM
CHUNK = 128
QK_CONV = 4
ATTN_HEADS = 8
ATTN_KV_HEADS = 2
ATTN_HEAD_DIM = 64
ATTN_W = ATTN_HEADS * ATTN_HEAD_DIM
ATTN_KV_W = ATTN_KV_HEADS * ATTN_HEAD_DIM
WINDOW = 128
ROPE_THETA = 10000.0
D_FF = 2816
FFN_CONV = 3
EPS = 1e-6
LOG2E = 1.4426950408889634

LANES = 128
SUBLANES = 8
GATE_W = LANES
IN_COLS = 2 * MLSTM_QK_W + 2 * MLSTM_W + ATTN_W + 2 * ATTN_KV_W + GATE_W

TM_MIX, TH_MIX = 1024, 512
TS_OUT = 512
TM_FFN = 1024
TF_FFN = 256
TR_FFN = 1024
VMEM_LIMIT = 56 * 1024 * 1024


def _sigmoid(x):
    return 1.0 / (1.0 + jnp.exp(-x))


def _log_sigmoid(x):
    return jnp.minimum(x, 0.0) - jnp.log1p(jnp.exp(-jnp.abs(x)))


def _rms(x, g):
    return x * lax.rsqrt(jnp.mean(x * x, axis=-1, keepdims=True) + EPS) * g


def _split3(x):
    hi = x.astype(BF16)
    r1 = x - hi.astype(F32)
    mid = r1.astype(BF16)
    lo = (r1 - mid.astype(F32)).astype(BF16)
    return hi, mid, lo


def _inproj_rows(x_ref, g_ref, w_ref, cos_ref, sin_ref, cw_ref, cb_ref, gb_ref, ng_ref,
                 q_ref, kt_ref, mv_ref, og_ref, gt_ref, aq_ref, akv_ref, hist_ref):
    ts = x_ref.shape[0]
    qkw = 2 * MLSTM_QK_W
    o_qk = 0
    o_v = o_qk + qkw
    o_o = o_v + MLSTM_W
    o_aq = o_o + MLSTM_W
    o_kv = o_aq + ATTN_W
    o_g = o_kv + 2 * ATTN_KV_W

    r_i = lax.broadcasted_iota(jnp.int32, (CHUNK, CHUNK), 0)
    c_i = lax.broadcasted_iota(jnp.int32, (CHUNK, CHUNK), 1)
    triu = (r_i <= c_i).astype(BF16)
    row8 = lax.broadcasted_iota(jnp.int32, (SUBLANES, ts), 0)
    sub = lax.broadcasted_iota(jnp.int32, (1, SUBLANES, qkw), 1)
    lane = lax.broadcasted_iota(jnp.int32, (ts, LANES), 1)
    first_half = (lane % ATTN_HEAD_DIM) < (ATTN_HEAD_DIM // 2)

    x = x_ref[...]
    h = (x * g_ref[...]).astype(BF16)
    rs = lax.rsqrt(jnp.mean(x * x, axis=-1, keepdims=True) + EPS)

    def proj(lo, width):
        return jnp.dot(h, w_ref[:, lo:lo + width], preferred_element_type=F32) * rs

    qk_pre = proj(o_qk, qkw)
    x3 = jnp.concatenate([hist_ref[...][None], qk_pre.reshape(ts // SUBLANES, SUBLANES, qkw)], axis=0)
    hist_ref[...] = qk_pre[ts - SUBLANES:, :]
    cur, prv = x3[1:], x3[:-1]
    y = cb_ref[...][None] + cw_ref[QK_CONV - 1:QK_CONV, :][None] * cur
    for j in range(1, QK_CONV):
        shifted = pltpu.roll(jnp.where(sub >= SUBLANES - j, prv, cur), j, axis=1)
        y = y + cw_ref[QK_CONV - 1 - j:QK_CONV - j, :][None] * shifted
    y = y.reshape(ts, qkw)
    act = y * _sigmoid(y)
    q_ref[...] = act[:, :MLSTM_QK_W].astype(BF16)
    kt_ref[...] = (act[:, MLSTM_QK_W:] * (MLSTM_QK_DIM ** -0.5)).T.astype(BF16)

    og_ref[...] = (ng_ref[...] * _sigmoid(proj(o_o, MLSTM_W))).astype(BF16)
    mv_ref[...] = proj(o_v, MLSTM_W).astype(BF16)

    c32 = cos_ref[...]
    s32 = sin_ref[...]
    cos = jnp.concatenate([c32] * 4, axis=-1)
    sin = jnp.concatenate([-s32, s32, -s32, s32], axis=-1)

    def rope(t):
        partner = jnp.where(first_half,
                            pltpu.roll(t, LANES - ATTN_HEAD_DIM // 2, axis=1),
                            pltpu.roll(t, ATTN_HEAD_DIM // 2, axis=1))
        return t * cos + partner * sin

    aq = proj(o_aq, ATTN_W)
    scale = ATTN_HEAD_DIM ** -0.5 * LOG2E
    for c in range(ATTN_W // LANES):
        sl = slice(c * LANES, (c + 1) * LANES)
        aq_ref[:, sl] = (rope(aq[:, sl]) * scale).astype(BF16)
    akv = proj(o_kv, 2 * ATTN_KV_W)
    akv_ref[:, :ATTN_KV_W] = rope(akv[:, :ATTN_KV_W]).astype(BF16)
    akv_ref[:, ATTN_KV_W:] = akv[:, ATTN_KV_W:].astype(BF16)

    gates = proj(o_g, GATE_W) + gb_ref[...]
    gt = gates.T[0:SUBLANES, :]
    comb = jnp.where(row8 < MLSTM_HEADS, gt, _log_sigmoid(gt))
    parts = _split3(comb)
    cums = []
    for c in range(ts // CHUNK):
        cs = slice(c * CHUNK, (c + 1) * CHUNK)
        acc = jnp.zeros((SUBLANES, CHUNK), F32)
        for part in parts:
            acc = acc + jnp.dot(part[:, cs], triu, preferred_element_type=F32)
        cums.append(acc)
    b = jnp.concatenate(cums, axis=1)
    gt_ref[...] = jnp.where(row8 < MLSTM_HEADS, comb - pltpu.roll(b, MLSTM_HEADS, axis=0), b)


def _mlstm_rows(q_ref, kt_ref, v_ref, og_ref, gt_ref, out_ref, c_ref, m_ref, nchunk):
    L = CHUNK
    H = MLSTM_HEADS
    DK = MLSTM_QK_DIM
    DV = MLSTM_V_DIM

    row_i = lax.broadcasted_iota(jnp.int32, (L, L), 0)
    col_i = lax.broadcasted_iota(jnp.int32, (L, L), 1)
    causal = col_i <= row_i
    eye = col_i == row_i
    qlane = lax.broadcasted_iota(jnp.int32, (L, H * DK), 1)
    row8 = lax.broadcasted_iota(jnp.int32, (SUBLANES, LANES), 0)
    ones_blk = jnp.ones((L, DV), BF16)
    neg_inf = jnp.float32(-jnp.inf)
    zero_bf = jnp.zeros((), BF16)

    for c in range(nchunk):
        rows = slice(c * L, (c + 1) * L)
        q = q_ref[rows, :]
        kt = kt_ref[:, rows]
        g8 = gt_ref[:, rows] * LOG2E
        m8 = m_ref[...]

        a_last8 = jnp.maximum(jnp.max(g8, axis=-1, keepdims=True), m8)
        decay8 = jnp.exp2(m8 - a_last8)
        ws8 = jnp.exp2(g8 - a_last8)
        b_last8 = pltpu.roll(g8, H, axis=0)[:, L - 1:L]
        m_ref[...] = jnp.where(row8 < H, b_last8 + a_last8, 0.0)

        qm = jnp.concatenate(
            [jnp.where((qlane >= h * DK) & (qlane < (h + 1) * DK), q, zero_bf) for h in range(H)], axis=0)
        sc = jnp.dot(qm, kt, preferred_element_type=F32)
        qc = jnp.dot(qm, c_ref[...].astype(BF16), preferred_element_type=F32)

        for h in range(H):
            hr = slice(h * L, (h + 1) * L)
            vs = slice(h * DV, (h + 1) * DV)
            ks = slice(h * DK, (h + 1) * DK)
            r_b = jnp.broadcast_to(g8[h:h + 1, :], (L, L))
            b_b = jnp.broadcast_to(g8[H + h:H + h + 1, :], (L, L))
            m_b = jnp.broadcast_to(m8[h:h + 1, :], (L, LANES))

            rmat = jnp.where(causal, r_b, neg_inf)
            a = jnp.maximum(jnp.max(rmat, axis=-1, keepdims=True), m_b)
            w_intra = jnp.exp2(rmat - a)
            w_inter = jnp.exp2(m_b - a)
            b_col = jnp.sum(jnp.where(eye, b_b, 0.0), axis=-1, keepdims=True)
            floor = jnp.exp2(-(b_col + a))

            s = (sc[hr, :] * w_intra).astype(BF16)
            v_ext = jnp.concatenate([v_ref[rows, vs], ones_blk], axis=-1)
            kw_t = (kt[ks, :].astype(F32) * ws8[h:h + 1, :]).astype(BF16)
            res = jnp.dot(jnp.concatenate([s, kw_t], axis=0), v_ext,
                          preferred_element_type=F32)
            num = w_inter * qc[hr, :DV] + res[:L, :DV]
            den = w_inter * qc[hr, DV:] + res[:L, DV:]
            hh = num / jnp.maximum(jnp.abs(den), floor)
            hn = hh * lax.rsqrt(jnp.mean(hh * hh, axis=-1, keepdims=True) + EPS)
            out_ref[rows, vs] = (hn * og_ref[rows, vs].astype(F32)).astype(BF16)

            dec = jnp.broadcast_to(decay8[h:h + 1, :], (DK, LANES))
            c_ref[ks, :] = jnp.concatenate([dec, dec], axis=-1) * c_ref[ks, :] + res[L:, :]


def _swa_rows(sink_ref, q_ref, kv_ref, kvp_ref, out_ref, nblk, l, not_first):
    W = WINDOW
    G = ATTN_HEADS // ATTN_KV_HEADS
    KW = ATTN_KV_W
    half = LANES // 2
    lane = lax.broadcasted_iota(jnp.int32, (W, LANES), 1)
    left = lane < half
    qpos = lax.broadcasted_iota(jnp.int32, (W, 2 * W), 0)
    kpos = lax.broadcasted_iota(jnp.int32, (W, 2 * W), 1)
    band = (kpos > qpos) & (kpos <= qpos + W)
    neg_inf = jnp.float32(-jnp.inf)
    zero = jnp.zeros((), BF16)

    for j in range(nblk):
        rows = slice(j * W, (j + 1) * W)
        if j == 0:
            kv2 = jnp.concatenate([kvp_ref[...], kv_ref[rows, :]], axis=0)
            valid = band & ((kpos >= W) | not_first)
        else:
            kv2 = kv_ref[(j - 1) * W:(j + 1) * W, :]
            valid = band
        k2 = kv2[:, :KW]
        v2 = kv2[:, KW:]
        qb = q_ref[rows, :]
        qs = jnp.concatenate(
            [jnp.where(left, qb[:, c * LANES:(c + 1) * LANES], zero) for c in range(G)]
            + [jnp.where(left, zero, qb[:, c * LANES:(c + 1) * LANES]) for c in range(G)], axis=0)
        s_all = lax.dot_general(qs, k2, (((1,), (1,)), ((), ())),
                                preferred_element_type=F32)
        ps, invs = [], []
        for h in range(ATTN_HEADS):
            sink = sink_ref[l, h] * LOG2E
            s = jnp.where(valid, s_all[h * W:(h + 1) * W, :], neg_inf)
            mx = jnp.maximum(jnp.max(s, axis=-1, keepdims=True), sink)
            p = jnp.exp2(s - mx)
            denom = jnp.sum(p, axis=-1, keepdims=True) + jnp.exp2(sink - mx)
            ps.append(p.astype(BF16))
            invs.append(1.0 / denom)
        pv = jnp.dot(jnp.concatenate(ps, axis=0), v2, preferred_element_type=F32)
        for c in range(G):
            lo = pv[c * W:(c + 1) * W, :] * invs[c]
            hi = pv[(G + c) * W:(G + c + 1) * W, :] * invs[G + c]
            out_ref[rows, c * LANES:(c + 1) * LANES] = jnp.where(left, lo, hi).astype(BF16)


def _mixer_kernel(sink_ref, x_ref, g_ref, w_ref, cos_ref, sin_ref, cw_ref, cb_ref, gb_ref, ng_ref,
                  wo_ref, gpost_ref, gpre_ref, x1_ref, h2_ref,
                  q_s, kt_s, v_s, og_s, gt_s, aq_s, akv_s, mo_s, ao_s, hist_ref, c_ref, m_ref, kvp_ref,
                  *, nhalf, nsub, l):
    tm = x_ref.shape[0]
    th = tm // nhalf
    first = pl.program_id(1) == 0

    @pl.when(first)
    def _():
        hist_ref[...] = jnp.zeros_like(hist_ref)
        c_ref[...] = jnp.zeros_like(c_ref)
        m_ref[...] = jnp.zeros_like(m_ref)
        kvp_ref[...] = jnp.zeros_like(kvp_ref)

    for s in range(nhalf):
        rows = pl.ds(s * th, th)
        _inproj_rows(x_ref.at[rows], g_ref, w_ref, cos_ref.at[rows], sin_ref.at[rows], cw_ref, cb_ref, gb_ref,
                     ng_ref, q_s.at[s], kt_s.at[s], v_s.at[s], og_s.at[s], gt_s.at[s], aq_s.at[s], akv_s.at[s],
                     hist_ref)
    for s in range(nhalf):
        rows = pl.ds(s * th, th)
        _mlstm_rows(q_s.at[s], kt_s.at[s], v_s.at[s], og_s.at[s], gt_s.at[s], mo_s.at[rows],
                    c_ref, m_ref, th // CHUNK)
        kvp = kvp_ref if s == 0 else akv_s.at[s - 1, pl.ds(th - WINDOW, WINDOW)]
        _swa_rows(sink_ref, aq_s.at[s], akv_s.at[s], kvp, ao_s.at[rows], th // WINDOW, l,
                  jnp.logical_not(first) if s == 0 else True)
    kvp_ref[...] = akv_s[nhalf - 1, th - WINDOW:, :]

    ts = tm // nsub
    for sb in range(nsub):
        rows = slice(sb * ts, (sb + 1) * ts)
        y = (jnp.dot(mo_s[rows, :], wo_ref[:MLSTM_W, :], preferred_element_type=F32)
             + jnp.dot(ao_s[rows, :], wo_ref[MLSTM_W:, :], preferred_element_type=F32))
        x1 = x_ref[rows, :] + _rms(y, gpost_ref[...])
        x1_ref[rows, :] = x1
        h2_ref[rows, :] = _rms(x1, gpre_ref[...]).astype(BF16)


def _mixer(sinks, x2, g, w, cos_t, sin_t, cw, cb, gb, ng, wo, gpost, gpre, batch, seq, l):
    T = x2.shape[0]
    tm = min(TM_MIX, seq)
    th = min(TH_MIX, tm)
    nb = seq // tm
    nhalf = tm // th
    row = lambda b, i: (b * nb + i, 0)
    pos = lambda b, i: (i, 0)
    layer = lambda b, i: (l, 0, 0)
    return pl.pallas_call(
        functools.partial(_mixer_kernel, nhalf=nhalf, nsub=tm // min(TS_OUT, tm), l=l),
        grid=(batch, nb),
        in_specs=[
            pl.BlockSpec(memory_space=pltpu.SMEM),
            pl.BlockSpec((tm, D_MODEL), row),
            pl.BlockSpec((None, 1, D_MODEL), layer),
            pl.BlockSpec((None, D_MODEL, IN_COLS), layer),
            pl.BlockSpec((tm, ATTN_HEAD_DIM // 2), pos),
            pl.BlockSpec((tm, ATTN_HEAD_DIM // 2), pos),
            pl.BlockSpec((None, QK_CONV, 2 * MLSTM_QK_W), layer),
            pl.BlockSpec((None, 1, 2 * MLSTM_QK_W), layer),
            pl.BlockSpec((None, 1, GATE_W), layer),
            pl.BlockSpec((None, 1, MLSTM_W), layer),
            pl.BlockSpec((None, MLSTM_W + ATTN_W, D_MODEL), layer),
            pl.BlockSpec((None, 1, D_MODEL), layer),
            pl.BlockSpec((None, 1, D_MODEL), layer),
        ],
        out_specs=[pl.BlockSpec((tm, D_MODEL), row), pl.BlockSpec((tm, D_MODEL), row)],
        out_shape=[jax.ShapeDtypeStruct((T, D_MODEL), F32), jax.ShapeDtypeStruct((T, D_MODEL), BF16)],
        scratch_shapes=[
            pltpu.VMEM((nhalf, th, MLSTM_QK_W), BF16),
            pltpu.VMEM((nhalf, MLSTM_QK_W, th), BF16),
            pltpu.VMEM((nhalf, th, MLSTM_W), BF16),
            pltpu.VMEM((nhalf, th, MLSTM_W), BF16),
            pltpu.VMEM((nhalf, SUBLANES, th), F32),
            pltpu.VMEM((nhalf, th, ATTN_W), BF16),
            pltpu.VMEM((nhalf, th, 2 * ATTN_KV_W), BF16),
            pltpu.VMEM((tm, MLSTM_W), BF16),
            pltpu.VMEM((tm, ATTN_W), BF16),
            pltpu.VMEM((SUBLANES, 2 * MLSTM_QK_W), F32),
            pltpu.VMEM((MLSTM_HEADS * MLSTM_QK_DIM, 2 * MLSTM_V_DIM), F32),
            pltpu.VMEM((SUBLANES, LANES), F32),
            pltpu.VMEM((WINDOW, 2 * ATTN_KV_W), BF16),
        ],
        compiler_params=pltpu.CompilerParams(
            dimension_semantics=("arbitrary", "arbitrary"), vmem_limit_bytes=VMEM_LIMIT),
        name="mixer",
    )(sinks, x2, g, w, cos_t, sin_t, cw, cb, gb, ng, wo, gpost, gpre)


def _ffn_kernel(h_ref, x_ref, wu_ref, wd_ref, cw_ref, gpost_ref, out_ref,
                hs_ref, act_ref, ua_ref, ub_ref, carry_g_ref, carry_v_ref, *, nf, tf, tiles_per_seq):
    tm = h_ref.shape[0]
    hs_ref[...] = h_ref[...]
    seq_start = (pl.program_id(0) % tiles_per_seq) == 0
    sub = lax.broadcasted_iota(jnp.int32, (1, SUBLANES, tf), 1)

    def conv(u, prev, cw):
        u3 = jnp.concatenate([prev[None], u.reshape(u.shape[0] // SUBLANES, SUBLANES, tf)], axis=0)
        cur, prv = u3[1:], u3[:-1]
        s1 = pltpu.roll(jnp.where(sub >= SUBLANES - 1, prv, cur), 1, axis=1)
        s2 = pltpu.roll(jnp.where(sub >= SUBLANES - 2, prv, cur), 2, axis=1)
        y = cw[2:3, :][None] * cur + cw[1:2, :][None] * s1 + cw[0:1, :][None] * s2
        return y.reshape(u.shape[0], tf)

    def gate_cols(f):
        return pl.ds(pl.multiple_of(f * tf, tf), tf)

    def val_cols(f):
        return pl.ds(pl.multiple_of(nf * tf + f * tf, tf), tf)

    tr = min(TR_FFN, tm)
    row_blocks = [slice(r * tr, (r + 1) * tr) for r in range(tm // tr)]

    def up(f, u_ref):
        for rs in row_blocks:
            h = hs_ref[rs, :]
            u_ref[rs, :tf] = jnp.dot(h, wu_ref[:, gate_cols(f)], preferred_element_type=F32)
            u_ref[rs, tf:] = jnp.dot(h, wu_ref[:, val_cols(f)], preferred_element_type=F32)

    def gate_act(f, u_ref):
        pg = jnp.where(seq_start, 0.0, carry_g_ref[f])
        pv = jnp.where(seq_start, 0.0, carry_v_ref[f])
        cwg = cw_ref[:, gate_cols(f)]
        cwv = cw_ref[:, val_cols(f)]
        for rs in row_blocks:
            ug = u_ref[rs, :tf]
            uv = u_ref[rs, tf:]
            gate = conv(ug, pg, cwg)
            val = conv(uv, pv, cwv)
            pg = ug[tr - SUBLANES:, :]
            pv = uv[tr - SUBLANES:, :]
            act_ref[rs, gate_cols(f)] = (gate * _sigmoid(gate) * val).astype(BF16)
        carry_g_ref[f] = pg
        carry_v_ref[f] = pv

    up(0, ua_ref)

    def body(i, carry):
        f = 2 * i
        up(f + 1, ub_ref)
        gate_act(f, ua_ref)
        up(f + 2, ua_ref)
        gate_act(f + 1, ub_ref)
        return carry

    lax.fori_loop(0, (nf - 1) // 2, body, 0, unroll=True)
    gate_act(nf - 1, ua_ref)

    y = jnp.dot(act_ref[...], wd_ref[...], preferred_element_type=F32)
    out_ref[...] = x_ref[...] + _rms(y, gpost_ref[...])


def _ffn(h2, x1, wu, wd, cw, gpost, seq, l):
    T = h2.shape[0]
    tm = min(TM_FFN, seq)
    tf = TF_FFN
    nf = D_FF // tf
    assert nf % 2 == 1 and nf * tf == D_FF
    row = lambda i: (i, 0)
    layer = lambda i: (l, 0, 0)
    return pl.pallas_call(
        functools.partial(_ffn_kernel, nf=nf, tf=tf, tiles_per_seq=seq // tm),
        grid=(T // tm,),
        in_specs=[
            pl.BlockSpec((tm, D_MODEL), row),
            pl.BlockSpec((tm, D_MODEL), row),
            pl.BlockSpec((None, D_MODEL, 2 * D_FF), layer),
            pl.BlockSpec((None, D_FF, D_MODEL), layer),
            pl.BlockSpec((None, FFN_CONV, 2 * D_FF), layer),
            pl.BlockSpec((None, 1, D_MODEL), layer),
        ],
        out_specs=pl.BlockSpec((tm, D_MODEL), row),
        out_shape=jax.ShapeDtypeStruct((T, D_MODEL), F32),
        scratch_shapes=[
            pltpu.VMEM((tm, D_MODEL), BF16),
            pltpu.VMEM((tm, D_FF), BF16),
            pltpu.VMEM((tm, 2 * tf), F32),
            pltpu.VMEM((tm, 2 * tf), F32),
            pltpu.VMEM((nf, SUBLANES, tf), F32),
            pltpu.VMEM((nf, SUBLANES, tf), F32),
        ],
        compiler_params=pltpu.CompilerParams(
            dimension_semantics=("arbitrary",), vmem_limit_bytes=VMEM_LIMIT),
        name="ffn",
    )(h2, x1, wu, wd, cw, gpost)


def _pair_heads(a, axis):
    G = ATTN_HEADS // ATTN_KV_HEADS
    shape = a.shape
    a = a.reshape(shape[:axis] + (ATTN_KV_HEADS, G, ATTN_HEAD_DIM) + shape[axis + 1:])
    a = jnp.swapaxes(a, axis, axis + 1)
    return a.reshape(shape)


def _rope_tables(seq):
    half = ATTN_HEAD_DIM // 2
    reps = LANES // half
    inv = 1.0 / (ROPE_THETA ** (jnp.arange(0, ATTN_HEAD_DIM, 2, dtype=F32) / ATTN_HEAD_DIM))
    pos = jnp.arange(seq, dtype=F32).reshape(seq // reps, reps, 1)
    ang = (pos * inv[None, None, :]).reshape(seq // reps, LANES)
    cos, sin = lax.optimization_barrier((jnp.cos(ang), jnp.sin(ang)))
    return cos.reshape(seq, half), sin.reshape(seq, half)


def kernel(x, g_pre_mix, w_in, qk_conv_w, qk_conv_b, gate_bias, mh_norm_g, attn_sinks, w_out,
           g_post_mix, g_pre_ffn, w_up, ffn_conv_w, w_down, g_post_ffn):
    batch, seq, _ = x.shape
    depth = w_in.shape[0]
    T = batch * seq
    cos_t, sin_t = _rope_tables(seq)

    n_gate = 2 * MLSTM_HEADS
    o_g = 2 * MLSTM_QK_W + 2 * MLSTM_W
    o_aq = o_g + n_gate
    o_ak = o_aq + ATTN_W
    w_a = jnp.concatenate([
        w_in[:, :, :o_g],
        _pair_heads(w_in[:, :, o_aq:o_ak], 2),
        w_in[:, :, o_ak:],
        w_in[:, :, o_g:o_aq],
        jnp.zeros((depth, D_MODEL, GATE_W - n_gate), w_in.dtype),
    ], axis=2).astype(BF16)
    gb = jnp.concatenate([gate_bias, jnp.zeros((depth, GATE_W - n_gate), F32)], axis=1)[:, None, :]
    w_o = jnp.concatenate([w_out[:, :MLSTM_W], _pair_heads(w_out[:, MLSTM_W:], 1)], axis=1).astype(BF16)
    w_u = w_up.astype(BF16)
    w_d = w_down.astype(BF16)
    vec = lambda p: p[:, None, :]

    x2 = x.reshape(T, D_MODEL)
    for l in range(depth):
        x1, h2 = _mixer(attn_sinks, x2, vec(g_pre_mix), w_a, cos_t, sin_t, qk_conv_w, vec(qk_conv_b), gb,
                        vec(mh_norm_g), w_o, vec(g_post_mix), vec(g_pre_ffn), batch, seq, l)
        x2 = _ffn(h2, x1, w_u, w_d, ffn_conv_w, vec(g_post_ffn), seq, l)
    return x2.reshape(batch, seq, D_MODEL)
```

```python
import functools

import jax
import jax.numpy as jnp
from jax import lax
from jax.experimental import pallas as pl
from jax.experimental.pallas import tpu as pltpu

F32 = jnp.float32
BF16 = jnp.bfloat16

D_MODEL = 1024
MLSTM_HEADS = 4
MLSTM_QK_DIM = 64
MLSTM_V_DIM = 128
MLSTM_QK_W = MLSTM_HEADS * MLSTM_QK_DIM
MLSTM_W = MLSTM_HEADS * MLSTM_V_DIM
CHUNK = 128
QK_CONV = 4
ATTN_HEADS = 8
ATTN_KV_HEADS = 2
ATTN_HEAD_DIM = 64
ATTN_W = ATTN_HEADS * ATTN_HEAD_DIM
ATTN_KV_W = ATTN_KV_HEADS * ATTN_HEAD_DIM
WINDOW = 128
ROPE_THETA = 10000.0
D_FF = 2816
FFN_CONV = 3
EPS = 1e-6
LOG2E = 1.4426950408889634

LANES = 128
SUBLANES = 8
GATE_W = LANES
IN_COLS = 2 * MLSTM_QK_W + 2 * MLSTM_W + ATTN_W + 2 * ATTN_KV_W + GATE_W

TM_MIX, TH_MIX = 1024, 512
TS_OUT = 512
TM_FFN = 1024
TF_FFN = 256
TR_FFN = 1024
VMEM_LIMIT = 56 * 1024 * 1024


def _sigmoid(x):
    return 1.0 / (1.0 + jnp.exp(-x))


def _log_sigmoid(x):
    return jnp.minimum(x, 0.0) - jnp.log1p(jnp.exp(-jnp.abs(x)))


def _rms(x, g):
    return x * lax.rsqrt(jnp.mean(x * x, axis=-1, keepdims=True) + EPS) * g


def _split3(x):
    hi = x.astype(BF16)
    r1 = x - hi.astype(F32)
    mid = r1.astype(BF16)
    lo = (r1 - mid.astype(F32)).astype(BF16)
    return hi, mid, lo


def _inproj_rows(x_ref, g_ref, w_ref, cos_ref, sin_ref, cw_ref, cb_ref, gb_ref, ng_ref,
                 q_ref, kt_ref, mv_ref, og_ref, gt_ref, aq_ref, akv_ref, hist_ref):
    ts = x_ref.shape[0]
    qkw = 2 * MLSTM_QK_W
    o_qk = 0
    o_v = o_qk + qkw
    o_o = o_v + MLSTM_W
    o_aq = o_o + MLSTM_W
    o_kv = o_aq + ATTN_W
    o_g = o_kv + 2 * ATTN_KV_W

    r_i = lax.broadcasted_iota(jnp.int32, (CHUNK, CHUNK), 0)
    c_i = lax.broadcasted_iota(jnp.int32, (CHUNK, CHUNK), 1)
    triu = (r_i <= c_i).astype(BF16)
    row8 = lax.broadcasted_iota(jnp.int32, (SUBLANES, ts), 0)
    sub = lax.broadcasted_iota(jnp.int32, (1, SUBLANES, qkw), 1)
    lane = lax.broadcasted_iota(jnp.int32, (ts, LANES), 1)
    first_half = (lane % ATTN_HEAD_DIM) < (ATTN_HEAD_DIM // 2)

    h = _rms(x_ref[...], g_ref[...]).astype(BF16)

    def proj(lo, width):
        return jnp.dot(h, w_ref[:, lo:lo + width], preferred_element_type=F32)

    qk_pre = proj(o_qk, qkw)
    x3 = jnp.concatenate([hist_ref[...][None], qk_pre.reshape(ts // SUBLANES, SUBLANES, qkw)], axis=0)
    hist_ref[...] = qk_pre[ts - SUBLANES:, :]
    cur, prv = x3[1:], x3[:-1]
    y = cb_ref[...][None] + cw_ref[QK_CONV - 1:QK_CONV, :][None] * cur
    for j in range(1, QK_CONV):
        shifted = pltpu.roll(jnp.where(sub >= SUBLANES - j, prv, cur), j, axis=1)
        y = y + cw_ref[QK_CONV - 1 - j:QK_CONV - j, :][None] * shifted
    y = y.reshape(ts, qkw)
    act = y * _sigmoid(y)
    q_ref[...] = act[:, :MLSTM_QK_W].astype(BF16)
    kt_ref[...] = (act[:, MLSTM_QK_W:] * (MLSTM_QK_DIM ** -0.5)).T.astype(BF16)

    og_ref[...] = (ng_ref[...] * _sigmoid(proj(o_o, MLSTM_W))).astype(BF16)
    mv_ref[...] = proj(o_v, MLSTM_W).astype(BF16)

    c32 = cos_ref[...]
    s32 = sin_ref[...]
    cos = jnp.concatenate([c32] * 4, axis=-1)
    sin = jnp.concatenate([-s32, s32, -s32, s32], axis=-1)

    def rope(t):
        partner = jnp.where(first_half,
                            pltpu.roll(t, LANES - ATTN_HEAD_DIM // 2, axis=1),
                            pltpu.roll(t, ATTN_HEAD_DIM // 2, axis=1))
        return t * cos + partner * sin

    aq = proj(o_aq, ATTN_W)
    scale = ATTN_HEAD_DIM ** -0.5 * LOG2E
    for c in range(ATTN_W // LANES):
        sl = slice(c * LANES, (c + 1) * LANES)
        aq_ref[:, sl] = (rope(aq[:, sl]) * scale).astype(BF16)
    akv = proj(o_kv, 2 * ATTN_KV_W)
    akv_ref[:, :ATTN_KV_W] = rope(akv[:, :ATTN_KV_W]).astype(BF16)
    akv_ref[:, ATTN_KV_W:] = akv[:, ATTN_KV_W:].astype(BF16)

    gates = proj(o_g, GATE_W) + gb_ref[...]
    gt = gates.T[0:SUBLANES, :]
    comb = jnp.where(row8 < MLSTM_HEADS, gt, _log_sigmoid(gt))
    parts = _split3(comb)
    cums = []
    for c in range(ts // CHUNK):
        cs = slice(c * CHUNK, (c + 1) * CHUNK)
        acc = jnp.zeros((SUBLANES, CHUNK), F32)
        for part in parts:
            acc = acc + jnp.dot(part[:, cs], triu, preferred_element_type=F32)
        cums.append(acc)
    b = jnp.concatenate(cums, axis=1)
    gt_ref[...] = jnp.where(row8 < MLSTM_HEADS, comb - pltpu.roll(b, MLSTM_HEADS, axis=0), b)


def _mlstm_rows(q_ref, kt_ref, v_ref, og_ref, gt_ref, out_ref, c_ref, m_ref, nchunk):
    L = CHUNK
    H = MLSTM_HEADS
    DK = MLSTM_QK_DIM
    DV = MLSTM_V_DIM

    row_i = lax.broadcasted_iota(jnp.int32, (L, L), 0)
    col_i = lax.broadcasted_iota(jnp.int32, (L, L), 1)
    causal = col_i <= row_i
    eye = col_i == row_i
    qlane = lax.broadcasted_iota(jnp.int32, (L, H * DK), 1)
    row8 = lax.broadcasted_iota(jnp.int32, (SUBLANES, LANES), 0)
    ones_blk = jnp.ones((L, DV), BF16)
    neg_inf = jnp.float32(-jnp.inf)
    zero_bf = jnp.zeros((), BF16)

    for c in range(nchunk):
        rows = slice(c * L, (c + 1) * L)
        q = q_ref[rows, :]
        kt = kt_ref[:, rows]
        g8 = gt_ref[:, rows] * LOG2E
        m8 = m_ref[...]

        a_last8 = jnp.maximum(jnp.max(g8, axis=-1, keepdims=True), m8)
        decay8 = jnp.exp2(m8 - a_last8)
        ws8 = jnp.exp2(g8 - a_last8)
        b_last8 = pltpu.roll(g8, H, axis=0)[:, L - 1:L]
        m_ref[...] = jnp.where(row8 < H, b_last8 + a_last8, 0.0)

        qm = jnp.concatenate(
            [jnp.where((qlane >= h * DK) & (qlane < (h + 1) * DK), q, zero_bf) for h in range(H)], axis=0)
        sc = jnp.dot(qm, kt, preferred_element_type=F32)
        qc = jnp.dot(qm, c_ref[...].astype(BF16), preferred_element_type=F32)

        for h in range(H):
            hr = slice(h * L, (h + 1) * L)
            vs = slice(h * DV, (h + 1) * DV)
            ks = slice(h * DK, (h + 1) * DK)
            r_b = jnp.broadcast_to(g8[h:h + 1, :], (L, L))
            b_b = jnp.broadcast_to(g8[H + h:H + h + 1, :], (L, L))
            m_b = jnp.broadcast_to(m8[h:h + 1, :], (L, LANES))

            rmat = jnp.where(causal, r_b, neg_inf)
            a = jnp.maximum(jnp.max(rmat, axis=-1, keepdims=True), m_b)
            w_intra = jnp.exp2(rmat - a)
            w_inter = jnp.exp2(m_b - a)
            b_col = jnp.sum(jnp.where(eye, b_b, 0.0), axis=-1, keepdims=True)
            floor = jnp.exp2(-(b_col + a))

            s = (sc[hr, :] * w_intra).astype(BF16)
            v_ext = jnp.concatenate([v_ref[rows, vs], ones_blk], axis=-1)
            kw_t = (kt[ks, :].astype(F32) * ws8[h:h + 1, :]).astype(BF16)
            res = jnp.dot(jnp.concatenate([s, kw_t], axis=0), v_ext,
                          preferred_element_type=F32)
            num = w_inter * qc[hr, :DV] + res[:L, :DV]
            den = w_inter * qc[hr, DV:] + res[:L, DV:]
            hh = num / jnp.maximum(jnp.abs(den), floor)
            hn = hh * lax.rsqrt(jnp.mean(hh * hh, axis=-1, keepdims=True) + EPS)
            out_ref[rows, vs] = (hn * og_ref[rows, vs].astype(F32)).astype(BF16)

            dec = jnp.broadcast_to(decay8[h:h + 1, :], (DK, LANES))
            c_ref[ks, :] = jnp.concatenate([dec, dec], axis=-1) * c_ref[ks, :] + res[L:, :]


def _swa_rows(sink_ref, q_ref, kv_ref, kvp_ref, out_ref, nblk, l, not_first):
    W = WINDOW
    G = ATTN_HEADS // ATTN_KV_HEADS
    KW = ATTN_KV_W
    half = LANES // 2
    lane = lax.broadcasted_iota(jnp.int32, (W, LANES), 1)
    left = lane < half
    qpos = lax.broadcasted_iota(jnp.int32, (W, 2 * W), 0)
    kpos = lax.broadcasted_iota(jnp.int32, (W, 2 * W), 1)
    band = (kpos > qpos) & (kpos <= qpos + W)
    neg_inf = jnp.float32(-jnp.inf)
    zero = jnp.zeros((), BF16)

    for j in range(nblk):
        rows = slice(j * W, (j + 1) * W)
        if j == 0:
            kv2 = jnp.concatenate([kvp_ref[...], kv_ref[rows, :]], axis=0)
            valid = band & ((kpos >= W) | not_first)
        else:
            kv2 = kv_ref[(j - 1) * W:(j + 1) * W, :]
            valid = band
        k2 = kv2[:, :KW]
        v2 = kv2[:, KW:]
        qb = q_ref[rows, :]
        qs = jnp.concatenate(
            [jnp.where(left, qb[:, c * LANES:(c + 1) * LANES], zero) for c in range(G)]
            + [jnp.where(left, zero, qb[:, c * LANES:(c + 1) * LANES]) for c in range(G)], axis=0)
        s_all = lax.dot_general(qs, k2, (((1,), (1,)), ((), ())),
                                preferred_element_type=F32)
        ps, invs = [], []
        for h in range(ATTN_HEADS):
            sink = sink_ref[l, h] * LOG2E
            s = jnp.where(valid, s_all[h * W:(h + 1) * W, :], neg_inf)
            mx = jnp.maximum(jnp.max(s, axis=-1, keepdims=True), sink)
            p = jnp.exp2(s - mx)
            denom = jnp.sum(p, axis=-1, keepdims=True) + jnp.exp2(sink - mx)
            ps.append(p.astype(BF16))
            invs.append(1.0 / denom)
        pv = jnp.dot(jnp.concatenate(ps, axis=0), v2, preferred_element_type=F32)
        for c in range(G):
            lo = pv[c * W:(c + 1) * W, :] * invs[c]
            hi = pv[(G + c) * W:(G + c + 1) * W, :] * invs[G + c]
            out_ref[rows, c * LANES:(c + 1) * LANES] = jnp.where(left, lo, hi).astype(BF16)


def _mixer_kernel(sink_ref, x_ref, g_ref, w_ref, cos_ref, sin_ref, cw_ref, cb_ref, gb_ref, ng_ref,
                  wo_ref, gpost_ref, gpre_ref, x1_ref, h2_ref,
                  q_s, kt_s, v_s, og_s, gt_s, aq_s, akv_s, mo_s, ao_s, hist_ref, c_ref, m_ref, kvp_ref,
                  *, nhalf, nsub, l):
    tm = x_ref.shape[0]
    th = tm // nhalf
    first = pl.program_id(1) == 0

    @pl.when(first)
    def _():
        hist_ref[...] = jnp.zeros_like(hist_ref)
        c_ref[...] = jnp.zeros_like(c_ref)
        m_ref[...] = jnp.zeros_like(m_ref)
        kvp_ref[...] = jnp.zeros_like(kvp_ref)

    for s in range(nhalf):
        rows = pl.ds(s * th, th)
        _inproj_rows(x_ref.at[rows], g_ref, w_ref, cos_ref.at[rows], sin_ref.at[rows], cw_ref, cb_ref, gb_ref,
                     ng_ref, q_s.at[s], kt_s.at[s], v_s.at[s], og_s.at[s], gt_s.at[s], aq_s.at[s], akv_s.at[s],
                     hist_ref)
    for s in range(nhalf):
        rows = pl.ds(s * th, th)
        _mlstm_rows(q_s.at[s], kt_s.at[s], v_s.at[s], og_s.at[s], gt_s.at[s], mo_s.at[rows],
                    c_ref, m_ref, th // CHUNK)
        kvp = kvp_ref if s == 0 else akv_s.at[s - 1, pl.ds(th - WINDOW, WINDOW)]
        _swa_rows(sink_ref, aq_s.at[s], akv_s.at[s], kvp, ao_s.at[rows], th // WINDOW, l,
                  jnp.logical_not(first) if s == 0 else True)
    kvp_ref[...] = akv_s[nhalf - 1, th - WINDOW:, :]

    ts = tm // nsub
    for sb in range(nsub):
        rows = slice(sb * ts, (sb + 1) * ts)
        y = (jnp.dot(mo_s[rows, :], wo_ref[:MLSTM_W, :], preferred_element_type=F32)
             + jnp.dot(ao_s[rows, :], wo_ref[MLSTM_W:, :], preferred_element_type=F32))
        x1 = x_ref[rows, :] + _rms(y, gpost_ref[...])
        x1_ref[rows, :] = x1
        h2_ref[rows, :] = _rms(x1, gpre_ref[...]).astype(BF16)


def _mixer(sinks, x2, g, w, cos_t, sin_t, cw, cb, gb, ng, wo, gpost, gpre, batch, seq, l):
    T = x2.shape[0]
    tm = min(TM_MIX, seq)
    th = min(TH_MIX, tm)
    nb = seq // tm
    nhalf = tm // th
    row = lambda b, i: (b * nb + i, 0)
    pos = lambda b, i: (i, 0)
    layer = lambda b, i: (l, 0, 0)
    return pl.pallas_call(
        functools.partial(_mixer_kernel, nhalf=nhalf, nsub=tm // min(TS_OUT, tm), l=l),
        grid=(batch, nb),
        in_specs=[
            pl.BlockSpec(memory_space=pltpu.SMEM),
            pl.BlockSpec((tm, D_MODEL), row),
            pl.BlockSpec((None, 1, D_MODEL), layer),
            pl.BlockSpec((None, D_MODEL, IN_COLS), layer),
            pl.BlockSpec((tm, ATTN_HEAD_DIM // 2), pos),
            pl.BlockSpec((tm, ATTN_HEAD_DIM // 2), pos),
            pl.BlockSpec((None, QK_CONV, 2 * MLSTM_QK_W), layer),
            pl.BlockSpec((None, 1, 2 * MLSTM_QK_W), layer),
            pl.BlockSpec((None, 1, GATE_W), layer),
            pl.BlockSpec((None, 1, MLSTM_W), layer),
            pl.BlockSpec((None, MLSTM_W + ATTN_W, D_MODEL), layer),
            pl.BlockSpec((None, 1, D_MODEL), layer),
            pl.BlockSpec((None, 1, D_MODEL), layer),
        ],
        out_specs=[pl.BlockSpec((tm, D_MODEL), row), pl.BlockSpec((tm, D_MODEL), row)],
        out_shape=[jax.ShapeDtypeStruct((T, D_MODEL), F32), jax.ShapeDtypeStruct((T, D_MODEL), BF16)],
        scratch_shapes=[
            pltpu.VMEM((nhalf, th, MLSTM_QK_W), BF16),
            pltpu.VMEM((nhalf, MLSTM_QK_W, th), BF16),
            pltpu.VMEM((nhalf, th, MLSTM_W), BF16),
            pltpu.VMEM((nhalf, th, MLSTM_W), BF16),
            pltpu.VMEM((nhalf, SUBLANES, th), F32),
            pltpu.VMEM((nhalf, th, ATTN_W), BF16),
            pltpu.VMEM((nhalf, th, 2 * ATTN_KV_W), BF16),
            pltpu.VMEM((tm, MLSTM_W), BF16),
            pltpu.VMEM((tm, ATTN_W), BF16),
            pltpu.VMEM((SUBLANES, 2 * MLSTM_QK_W), F32),
            pltpu.VMEM((MLSTM_HEADS * MLSTM_QK_DIM, 2 * MLSTM_V_DIM), F32),
            pltpu.VMEM((SUBLANES, LANES), F32),
            pltpu.VMEM((WINDOW, 2 * ATTN_KV_W), BF16),
        ],
        compiler_params=pltpu.CompilerParams(
            dimension_semantics=("arbitrary", "arbitrary"), vmem_limit_bytes=VMEM_LIMIT),
        name="mixer",
    )(sinks, x2, g, w, cos_t, sin_t, cw, cb, gb, ng, wo, gpost, gpre)


def _ffn_kernel(h_ref, x_ref, wu_ref, wd_ref, cw_ref, gpost_ref, out_ref,
                hs_ref, act_ref, ua_ref, ub_ref, carry_g_ref, carry_v_ref, *, nf, tf, tiles_per_seq):
    tm = h_ref.shape[0]
    hs_ref[...] = h_ref[...]
    seq_start = (pl.program_id(0) % tiles_per_seq) == 0
    sub = lax.broadcasted_iota(jnp.int32, (1, SUBLANES, tf), 1)

    def conv(u, prev, cw):
        u3 = jnp.concatenate([prev[None], u.reshape(u.shape[0] // SUBLANES, SUBLANES, tf)], axis=0)
        cur, prv = u3[1:], u3[:-1]
        s1 = pltpu.roll(jnp.where(sub >= SUBLANES - 1, prv, cur), 1, axis=1)
        s2 = pltpu.roll(jnp.where(sub >= SUBLANES - 2, prv, cur), 2, axis=1)
        y = cw[2:3, :][None] * cur + cw[1:2, :][None] * s1 + cw[0:1, :][None] * s2
        return y.reshape(u.shape[0], tf)

    def gate_cols(f):
        return pl.ds(pl.multiple_of(f * tf, tf), tf)

    def val_cols(f):
        return pl.ds(pl.multiple_of(nf * tf + f * tf, tf), tf)

    tr = min(TR_FFN, tm)
    row_blocks = [slice(r * tr, (r + 1) * tr) for r in range(tm // tr)]

    def up(f, u_ref):
        for rs in row_blocks:
            h = hs_ref[rs, :]
            u_ref[rs, :tf] = jnp.dot(h, wu_ref[:, gate_cols(f)], preferred_element_type=F32)
            u_ref[rs, tf:] = jnp.dot(h, wu_ref[:, val_cols(f)], preferred_element_type=F32)

    def gate_act(f, u_ref):
        pg = jnp.where(seq_start, 0.0, carry_g_ref[f])
        pv = jnp.where(seq_start, 0.0, carry_v_ref[f])
        cwg = cw_ref[:, gate_cols(f)]
        cwv = cw_ref[:, val_cols(f)]
        for rs in row_blocks:
            ug = u_ref[rs, :tf]
            uv = u_ref[rs, tf:]
            gate = conv(ug, pg, cwg)
            val = conv(uv, pv, cwv)
            pg = ug[tr - SUBLANES:, :]
            pv = uv[tr - SUBLANES:, :]
            act_ref[rs, gate_cols(f)] = (gate * _sigmoid(gate) * val).astype(BF16)
        carry_g_ref[f] = pg
        carry_v_ref[f] = pv

    up(0, ua_ref)

    def body(i, carry):
        f = 2 * i
        up(f + 1, ub_ref)
        gate_act(f, ua_ref)
        up(f + 2, ua_ref)
        gate_act(f + 1, ub_ref)
        return carry

    lax.fori_loop(0, (nf - 1) // 2, body, 0, unroll=True)
    gate_act(nf - 1, ua_ref)

    for rs in (slice(0, tm // 2), slice(tm // 2, tm)):
        y = jnp.dot(act_ref[rs, :], wd_ref[...], preferred_element_type=F32)
        out_ref[rs, :] = x_ref[rs, :] + _rms(y, gpost_ref[...])


def _ffn(h2, x1, wu, wd, cw, gpost, seq, l):
    T = h2.shape[0]
    tm = min(TM_FFN, seq)
    tf = TF_FFN
    nf = D_FF // tf
    assert nf % 2 == 1 and nf * tf == D_FF
    row = lambda i: (i, 0)
    layer = lambda i: (l, 0, 0)
    return pl.pallas_call(
        functools.partial(_ffn_kernel, nf=nf, tf=tf, tiles_per_seq=seq // tm),
        grid=(T // tm,),
        in_specs=[
            pl.BlockSpec((tm, D_MODEL), row),
            pl.BlockSpec((tm, D_MODEL), row),
            pl.BlockSpec((None, D_MODEL, 2 * D_FF), layer),
            pl.BlockSpec((None, D_FF, D_MODEL), layer),
            pl.BlockSpec((None, FFN_CONV, 2 * D_FF), layer),
            pl.BlockSpec((None, 1, D_MODEL), layer),
        ],
        out_specs=pl.BlockSpec((tm, D_MODEL), row),
        out_shape=jax.ShapeDtypeStruct((T, D_MODEL), F32),
        scratch_shapes=[
            pltpu.VMEM((tm, D_MODEL), BF16),
            pltpu.VMEM((tm, D_FF), BF16),
            pltpu.VMEM((tm, 2 * tf), F32),
            pltpu.VMEM((tm, 2 * tf), F32),
            pltpu.VMEM((nf, SUBLANES, tf), F32),
            pltpu.VMEM((nf, SUBLANES, tf), F32),
        ],
        compiler_params=pltpu.CompilerParams(
            dimension_semantics=("arbitrary",), vmem_limit_bytes=VMEM_LIMIT),
        name="ffn",
    )(h2, x1, wu, wd, cw, gpost)


def _pair_heads(a, axis):
    G = ATTN_HEADS // ATTN_KV_HEADS
    shape = a.shape
    a = a.reshape(shape[:axis] + (ATTN_KV_HEADS, G, ATTN_HEAD_DIM) + shape[axis + 1:])
    a = jnp.swapaxes(a, axis, axis + 1)
    return a.reshape(shape)


def _rope_tables(seq):
    half = ATTN_HEAD_DIM // 2
    reps = LANES // half
    inv = 1.0 / (ROPE_THETA ** (jnp.arange(0, ATTN_HEAD_DIM, 2, dtype=F32) / ATTN_HEAD_DIM))
    pos = jnp.arange(seq, dtype=F32).reshape(seq // reps, reps, 1)
    ang = (pos * inv[None, None, :]).reshape(seq // reps, LANES)
    cos, sin = lax.optimization_barrier((jnp.cos(ang), jnp.sin(ang)))
    return cos.reshape(seq, half), sin.reshape(seq, half)


def kernel(x, g_pre_mix, w_in, qk_conv_w, qk_conv_b, gate_bias, mh_norm_g, attn_sinks, w_out,
           g_post_mix, g_pre_ffn, w_up, ffn_conv_w, w_down, g_post_ffn):
    batch, seq, _ = x.shape
    depth = w_in.shape[0]
    T = batch * seq
    cos_t, sin_t = _rope_tables(seq)

    n_gate = 2 * MLSTM_HEADS
    o_g = 2 * MLSTM_QK_W + 2 * MLSTM_W
    o_aq = o_g + n_gate
    o_ak = o_aq + ATTN_W
    w_a = jnp.concatenate([
        w_in[:, :, :o_g],
        _pair_heads(w_in[:, :, o_aq:o_ak], 2),
        w_in[:, :, o_ak:],
        w_in[:, :, o_g:o_aq],
        jnp.zeros((depth, D_MODEL, GATE_W - n_gate), w_in.dtype),
    ], axis=2).astype(BF16)
    gb = jnp.concatenate([gate_bias, jnp.zeros((depth, GATE_W - n_gate), F32)], axis=1)[:, None, :]
    w_o = jnp.concatenate([w_out[:, :MLSTM_W], _pair_heads(w_out[:, MLSTM_W:], 1)], axis=1).astype(BF16)
    w_u = w_up.astype(BF16)
    w_d = w_down.astype(BF16)
    vec = lambda p: p[:, None, :]

    x2 = x.reshape(T, D_MODEL)
    for l in range(depth):
        x1, h2 = _mixer(attn_sinks, x2, vec(g_pre_mix), w_a, cos_t, sin_t, qk_conv_w, vec(qk_conv_b), gb,
                        vec(mh_norm_g), w_o, vec(g_post_mix), vec(g_pre_ffn), batch, seq, l)
        x2 = _ffn(h2, x1, w_u, w_d, ffn_conv_w, vec(g_post_ffn), seq, l)
    return x2.reshape(batch, seq, D_MODEL)
```

```python
import functools

import jax
import jax.numpy as jnp
from jax import lax
from jax.experimental import pallas as pl
from jax.experimental.pallas import tpu as pltpu

F32 = jnp.float32
BF16 = jnp.bfloat16

D_MODEL = 1024
MLSTM_HEADS = 4
MLSTM_QK_DIM = 64
MLSTM_V_DIM = 128
MLSTM_QK_W = MLSTM_HEADS * MLSTM_QK_DIM
MLSTM_W = MLSTM_HEADS * MLSTM_V_DIM
CHUNK = 128
QK_CONV = 4
ATTN_HEADS = 8
ATTN_KV_HEADS = 2
ATTN_HEAD_DIM = 64
ATTN_W = ATTN_HEADS * ATTN_HEAD_DIM
ATTN_KV_W = ATTN_KV_HEADS * ATTN_HEAD_DIM
WINDOW = 128
ROPE_THETA = 10000.0
D_FF = 2816
FFN_CONV = 3
EPS = 1e-6
LOG2E = 1.4426950408889634

LANES = 128
SUBLANES = 8
GATE_W = LANES
IN_COLS = 2 * MLSTM_QK_W + 2 * MLSTM_W + ATTN_W + 2 * ATTN_KV_W + GATE_W

TM_MIX, TH_MIX = 1024, 512
TS_OUT = 512
TM_FFN = 1024
TF_FFN = 256
TR_FFN = 1024
ND_FFN = 4
VMEM_LIMIT = 56 * 1024 * 1024


def _sigmoid(x):
    return 1.0 / (1.0 + jnp.exp(-x))


def _log_sigmoid(x):
    return jnp.minimum(x, 0.0) - jnp.log1p(jnp.exp(-jnp.abs(x)))


def _rms(x, g):
    return x * lax.rsqrt(jnp.mean(x * x, axis=-1, keepdims=True) + EPS) * g


def _split3(x):
    hi = x.astype(BF16)
    r1 = x - hi.astype(F32)
    mid = r1.astype(BF16)
    lo = (r1 - mid.astype(F32)).astype(BF16)
    return hi, mid, lo


def _inproj_rows(x_ref, g_ref, w_ref, cos_ref, sin_ref, cw_ref, cb_ref, gb_ref, ng_ref,
                 q_ref, kt_ref, mv_ref, og_ref, gt_ref, aq_ref, akv_ref, hist_ref):
    ts = x_ref.shape[0]
    qkw = 2 * MLSTM_QK_W
    o_qk = 0
    o_v = o_qk + qkw
    o_o = o_v + MLSTM_W
    o_aq = o_o + MLSTM_W
    o_kv = o_aq + ATTN_W
    o_g = o_kv + 2 * ATTN_KV_W

    r_i = lax.broadcasted_iota(jnp.int32, (CHUNK, CHUNK), 0)
    c_i = lax.broadcasted_iota(jnp.int32, (CHUNK, CHUNK), 1)
    triu = (r_i <= c_i).astype(BF16)
    row8 = lax.broadcasted_iota(jnp.int32, (SUBLANES, ts), 0)
    sub = lax.broadcasted_iota(jnp.int32, (1, SUBLANES, qkw), 1)
    lane = lax.broadcasted_iota(jnp.int32, (ts, LANES), 1)
    first_half = (lane % ATTN_HEAD_DIM) < (ATTN_HEAD_DIM // 2)

    h = _rms(x_ref[...], g_ref[...]).astype(BF16)

    def proj(lo, width):
        return jnp.dot(h, w_ref[:, lo:lo + width], preferred_element_type=F32)

    qk_pre = proj(o_qk, qkw)
    x3 = jnp.concatenate([hist_ref[...][None], qk_pre.reshape(ts // SUBLANES, SUBLANES, qkw)], axis=0)
    hist_ref[...] = qk_pre[ts - SUBLANES:, :]
    cur, prv = x3[1:], x3[:-1]
    y = cb_ref[...][None] + cw_ref[QK_CONV - 1:QK_CONV, :][None] * cur
    for j in range(1, QK_CONV):
        shifted = pltpu.roll(jnp.where(sub >= SUBLANES - j, prv, cur), j, axis=1)
        y = y + cw_ref[QK_CONV - 1 - j:QK_CONV - j, :][None] * shifted
    y = y.reshape(ts, qkw)
    act = y * _sigmoid(y)
    q_ref[...] = act[:, :MLSTM_QK_W].astype(BF16)
    kt_ref[...] = (act[:, MLSTM_QK_W:] * (MLSTM_QK_DIM ** -0.5)).T.astype(BF16)

    og_ref[...] = (ng_ref[...] * _sigmoid(proj(o_o, MLSTM_W))).astype(BF16)
    mv_ref[...] = proj(o_v, MLSTM_W).astype(BF16)

    c32 = cos_ref[...]
    s32 = sin_ref[...]
    cos = jnp.concatenate([c32] * 4, axis=-1)
    sin = jnp.concatenate([-s32, s32, -s32, s32], axis=-1)

    def rope(t):
        partner = jnp.where(first_half,
                            pltpu.roll(t, LANES - ATTN_HEAD_DIM // 2, axis=1),
                            pltpu.roll(t, ATTN_HEAD_DIM // 2, axis=1))
        return t * cos + partner * sin

    aq = proj(o_aq, ATTN_W)
    scale = ATTN_HEAD_DIM ** -0.5 * LOG2E
    for c in range(ATTN_W // LANES):
        sl = slice(c * LANES, (c + 1) * LANES)
        aq_ref[:, sl] = (rope(aq[:, sl]) * scale).astype(BF16)
    akv = proj(o_kv, 2 * ATTN_KV_W)
    akv_ref[:, :ATTN_KV_W] = rope(akv[:, :ATTN_KV_W]).astype(BF16)
    akv_ref[:, ATTN_KV_W:] = akv[:, ATTN_KV_W:].astype(BF16)

    gates = proj(o_g, GATE_W) + gb_ref[...]
    gt = gates.T[0:SUBLANES, :]
    comb = jnp.where(row8 < MLSTM_HEADS, gt, _log_sigmoid(gt))
    parts = _split3(comb)
    cums = []
    for c in range(ts // CHUNK):
        cs = slice(c * CHUNK, (c + 1) * CHUNK)
        acc = jnp.zeros((SUBLANES, CHUNK), F32)
        for part in parts:
            acc = acc + jnp.dot(part[:, cs], triu, preferred_element_type=F32)
        cums.append(acc)
    b = jnp.concatenate(cums, axis=1)
    gt_ref[...] = jnp.where(row8 < MLSTM_HEADS, comb - pltpu.roll(b, MLSTM_HEADS, axis=0), b)


def _mlstm_rows(q_ref, kt_ref, v_ref, og_ref, gt_ref, out_ref, c_ref, m_ref, nchunk):
    L = CHUNK
    H = MLSTM_HEADS
    DK = MLSTM_QK_DIM
    DV = MLSTM_V_DIM

    row_i = lax.broadcasted_iota(jnp.int32, (L, L), 0)
    col_i = lax.broadcasted_iota(jnp.int32, (L, L), 1)
    causal = col_i <= row_i
    eye = col_i == row_i
    qlane = lax.broadcasted_iota(jnp.int32, (L, H * DK), 1)
    row8 = lax.broadcasted_iota(jnp.int32, (SUBLANES, LANES), 0)
    ones_blk = jnp.ones((L, DV), BF16)
    neg_inf = jnp.float32(-jnp.inf)
    zero_bf = jnp.zeros((), BF16)

    for c in range(nchunk):
        rows = slice(c * L, (c + 1) * L)
        q = q_ref[rows, :]
        kt = kt_ref[:, rows]
        g8 = gt_ref[:, rows] * LOG2E
        m8 = m_ref[...]

        a_last8 = jnp.maximum(jnp.max(g8, axis=-1, keepdims=True), m8)
        decay8 = jnp.exp2(m8 - a_last8)
        ws8 = jnp.exp2(g8 - a_last8)
        b_last8 = pltpu.roll(g8, H, axis=0)[:, L - 1:L]
        m_ref[...] = jnp.where(row8 < H, b_last8 + a_last8, 0.0)

        qm = jnp.concatenate(
            [jnp.where((qlane >= h * DK) & (qlane < (h + 1) * DK), q, zero_bf) for h in range(H)], axis=0)
        sc = jnp.dot(qm, kt, preferred_element_type=F32)
        qc = jnp.dot(qm, c_ref[...].astype(BF16), preferred_element_type=F32)

        for h in range(H):
            hr = slice(h * L, (h + 1) * L)
            vs = slice(h * DV, (h + 1) * DV)
            ks = slice(h * DK, (h + 1) * DK)
            r_b = jnp.broadcast_to(g8[h:h + 1, :], (L, L))
            b_b = jnp.broadcast_to(g8[H + h:H + h + 1, :], (L, L))
            m_b = jnp.broadcast_to(m8[h:h + 1, :], (L, LANES))

            rmat = jnp.where(causal, r_b, neg_inf)
            a = jnp.maximum(jnp.max(rmat, axis=-1, keepdims=True), m_b)
            w_intra = jnp.exp2(rmat - a)
            w_inter = jnp.exp2(m_b - a)
            b_col = jnp.sum(jnp.where(eye, b_b, 0.0), axis=-1, keepdims=True)
            floor = jnp.exp2(-(b_col + a))

            s = (sc[hr, :] * w_intra).astype(BF16)
            v_ext = jnp.concatenate([v_ref[rows, vs], ones_blk], axis=-1)
            kw_t = (kt[ks, :].astype(F32) * ws8[h:h + 1, :]).astype(BF16)
            res = jnp.dot(jnp.concatenate([s, kw_t], axis=0), v_ext,
                          preferred_element_type=F32)
            num = w_inter * qc[hr, :DV] + res[:L, :DV]
            den = w_inter * qc[hr, DV:] + res[:L, DV:]
            hh = num / jnp.maximum(jnp.abs(den), floor)
            hn = hh * lax.rsqrt(jnp.mean(hh * hh, axis=-1, keepdims=True) + EPS)
            out_ref[rows, vs] = (hn * og_ref[rows, vs].astype(F32)).astype(BF16)

            dec = jnp.broadcast_to(decay8[h:h + 1, :], (DK, LANES))
            c_ref[ks, :] = jnp.concatenate([dec, dec], axis=-1) * c_ref[ks, :] + res[L:, :]


def _swa_rows(sink_ref, q_ref, kv_ref, kvp_ref, out_ref, nblk, l, not_first):
    W = WINDOW
    G = ATTN_HEADS // ATTN_KV_HEADS
    KW = ATTN_KV_W
    half = LANES // 2
    lane = lax.broadcasted_iota(jnp.int32, (W, LANES), 1)
    left = lane < half
    qpos = lax.broadcasted_iota(jnp.int32, (W, 2 * W), 0)
    kpos = lax.broadcasted_iota(jnp.int32, (W, 2 * W), 1)
    band = (kpos > qpos) & (kpos <= qpos + W)
    neg_inf = jnp.float32(-jnp.inf)
    zero = jnp.zeros((), BF16)

    for j in range(nblk):
        rows = slice(j * W, (j + 1) * W)
        if j == 0:
            kv2 = jnp.concatenate([kvp_ref[...], kv_ref[rows, :]], axis=0)
            valid = band & ((kpos >= W) | not_first)
        else:
            kv2 = kv_ref[(j - 1) * W:(j + 1) * W, :]
            valid = band
        k2 = kv2[:, :KW]
        v2 = kv2[:, KW:]
        qb = q_ref[rows, :]
        qs = jnp.concatenate(
            [jnp.where(left, qb[:, c * LANES:(c + 1) * LANES], zero) for c in range(G)]
            + [jnp.where(left, zero, qb[:, c * LANES:(c + 1) * LANES]) for c in range(G)], axis=0)
        s_all = lax.dot_general(qs, k2, (((1,), (1,)), ((), ())),
                                preferred_element_type=F32)
        ps, invs = [], []
        for h in range(ATTN_HEADS):
            sink = sink_ref[l, h] * LOG2E
            s = jnp.where(valid, s_all[h * W:(h + 1) * W, :], neg_inf)
            mx = jnp.maximum(jnp.max(s, axis=-1, keepdims=True), sink)
            p = jnp.exp2(s - mx)
            denom = jnp.sum(p, axis=-1, keepdims=True) + jnp.exp2(sink - mx)
            ps.append(p.astype(BF16))
            invs.append(1.0 / denom)
        pv = jnp.dot(jnp.concatenate(ps, axis=0), v2, preferred_element_type=F32)
        for c in range(G):
            lo = pv[c * W:(c + 1) * W, :] * invs[c]
            hi = pv[(G + c) * W:(G + c + 1) * W, :] * invs[G + c]
            out_ref[rows, c * LANES:(c + 1) * LANES] = jnp.where(left, lo, hi).astype(BF16)


def _mixer_kernel(sink_ref, x_ref, g_ref, w_ref, cos_ref, sin_ref, cw_ref, cb_ref, gb_ref, ng_ref,
                  wo_ref, gpost_ref, gpre_ref, x1_ref, h2_ref,
                  q_s, kt_s, v_s, og_s, gt_s, aq_s, akv_s, mo_s, ao_s, hist_ref, c_ref, m_ref, kvp_ref,
                  *, nhalf, nsub, l):
    tm = x_ref.shape[0]
    th = tm // nhalf
    first = pl.program_id(1) == 0

    @pl.when(first)
    def _():
        hist_ref[...] = jnp.zeros_like(hist_ref)
        c_ref[...] = jnp.zeros_like(c_ref)
        m_ref[...] = jnp.zeros_like(m_ref)
        kvp_ref[...] = jnp.zeros_like(kvp_ref)

    for s in range(nhalf):
        rows = pl.ds(s * th, th)
        _inproj_rows(x_ref.at[rows], g_ref, w_ref, cos_ref.at[rows], sin_ref.at[rows], cw_ref, cb_ref, gb_ref,
                     ng_ref, q_s.at[s], kt_s.at[s], v_s.at[s], og_s.at[s], gt_s.at[s], aq_s.at[s], akv_s.at[s],
                     hist_ref)
    for s in range(nhalf):
        rows = pl.ds(s * th, th)
        _mlstm_rows(q_s.at[s], kt_s.at[s], v_s.at[s], og_s.at[s], gt_s.at[s], mo_s.at[rows],
                    c_ref, m_ref, th // CHUNK)
        kvp = kvp_ref if s == 0 else akv_s.at[s - 1, pl.ds(th - WINDOW, WINDOW)]
        _swa_rows(sink_ref, aq_s.at[s], akv_s.at[s], kvp, ao_s.at[rows], th // WINDOW, l,
                  jnp.logical_not(first) if s == 0 else True)
    kvp_ref[...] = akv_s[nhalf - 1, th - WINDOW:, :]

    ts = tm // nsub
    for sb in range(nsub):
        rows = slice(sb * ts, (sb + 1) * ts)
        y = (jnp.dot(mo_s[rows, :], wo_ref[:MLSTM_W, :], preferred_element_type=F32)
             + jnp.dot(ao_s[rows, :], wo_ref[MLSTM_W:, :], preferred_element_type=F32))
        x1 = x_ref[rows, :] + _rms(y, gpost_ref[...])
        x1_ref[rows, :] = x1
        h2_ref[rows, :] = _rms(x1, gpre_ref[...]).astype(BF16)


def _mixer(sinks, x2, g, w, cos_t, sin_t, cw, cb, gb, ng, wo, gpost, gpre, batch, seq, l):
    T = x2.shape[0]
    tm = min(TM_MIX, seq)
    th = min(TH_MIX, tm)
    nb = seq // tm
    nhalf = tm // th
    row = lambda b, i: (b * nb + i, 0)
    pos = lambda b, i: (i, 0)
    layer = lambda b, i: (l, 0, 0)
    return pl.pallas_call(
        functools.partial(_mixer_kernel, nhalf=nhalf, nsub=tm // min(TS_OUT, tm), l=l),
        grid=(batch, nb),
        in_specs=[
            pl.BlockSpec(memory_space=pltpu.SMEM),
            pl.BlockSpec((tm, D_MODEL), row),
            pl.BlockSpec((None, 1, D_MODEL), layer),
            pl.BlockSpec((None, D_MODEL, IN_COLS), layer),
            pl.BlockSpec((tm, ATTN_HEAD_DIM // 2), pos),
            pl.BlockSpec((tm, ATTN_HEAD_DIM // 2), pos),
            pl.BlockSpec((None, QK_CONV, 2 * MLSTM_QK_W), layer),
            pl.BlockSpec((None, 1, 2 * MLSTM_QK_W), layer),
            pl.BlockSpec((None, 1, GATE_W), layer),
            pl.BlockSpec((None, 1, MLSTM_W), layer),
            pl.BlockSpec((None, MLSTM_W + ATTN_W, D_MODEL), layer),
            pl.BlockSpec((None, 1, D_MODEL), layer),
            pl.BlockSpec((None, 1, D_MODEL), layer),
        ],
        out_specs=[pl.BlockSpec((tm, D_MODEL), row), pl.BlockSpec((tm, D_MODEL), row)],
        out_shape=[jax.ShapeDtypeStruct((T, D_MODEL), F32), jax.ShapeDtypeStruct((T, D_MODEL), BF16)],
        scratch_shapes=[
            pltpu.VMEM((nhalf, th, MLSTM_QK_W), BF16),
            pltpu.VMEM((nhalf, MLSTM_QK_W, th), BF16),
            pltpu.VMEM((nhalf, th, MLSTM_W), BF16),
            pltpu.VMEM((nhalf, th, MLSTM_W), BF16),
            pltpu.VMEM((nhalf, SUBLANES, th), F32),
            pltpu.VMEM((nhalf, th, ATTN_W), BF16),
            pltpu.VMEM((nhalf, th, 2 * ATTN_KV_W), BF16),
            pltpu.VMEM((tm, MLSTM_W), BF16),
            pltpu.VMEM((tm, ATTN_W), BF16),
            pltpu.VMEM((SUBLANES, 2 * MLSTM_QK_W), F32),
            pltpu.VMEM((MLSTM_HEADS * MLSTM_QK_DIM, 2 * MLSTM_V_DIM), F32),
            pltpu.VMEM((SUBLANES, LANES), F32),
            pltpu.VMEM((WINDOW, 2 * ATTN_KV_W), BF16),
        ],
        compiler_params=pltpu.CompilerParams(
            dimension_semantics=("arbitrary", "arbitrary"), vmem_limit_bytes=VMEM_LIMIT),
        name="mixer",
    )(sinks, x2, g, w, cos_t, sin_t, cw, cb, gb, ng, wo, gpost, gpre)


def _ffn_kernel(h_ref, x_ref, wu_ref, wd_ref, cw_ref, gpost_ref, out_ref,
                hs_ref, act_ref, ua_ref, ub_ref, carry_g_ref, carry_v_ref, *, nf, tf, tiles_per_seq):
    tm = h_ref.shape[0]
    hs_ref[...] = h_ref[...]
    seq_start = (pl.program_id(0) % tiles_per_seq) == 0
    sub = lax.broadcasted_iota(jnp.int32, (1, SUBLANES, tf), 1)

    def conv(u, prev, cw):
        u3 = jnp.concatenate([prev[None], u.reshape(u.shape[0] // SUBLANES, SUBLANES, tf)], axis=0)
        cur, prv = u3[1:], u3[:-1]
        s1 = pltpu.roll(jnp.where(sub >= SUBLANES - 1, prv, cur), 1, axis=1)
        s2 = pltpu.roll(jnp.where(sub >= SUBLANES - 2, prv, cur), 2, axis=1)
        y = cw[2:3, :][None] * cur + cw[1:2, :][None] * s1 + cw[0:1, :][None] * s2
        return y.reshape(u.shape[0], tf)

    def gate_cols(f):
        return pl.ds(pl.multiple_of(f * tf, tf), tf)

    def val_cols(f):
        return pl.ds(pl.multiple_of(nf * tf + f * tf, tf), tf)

    tr = min(TR_FFN, tm)
    row_blocks = [slice(r * tr, (r + 1) * tr) for r in range(tm // tr)]

    def up(f, u_ref):
        for rs in row_blocks:
            h = hs_ref[rs, :]
            u_ref[rs, :tf] = jnp.dot(h, wu_ref[:, gate_cols(f)], preferred_element_type=F32)
            u_ref[rs, tf:] = jnp.dot(h, wu_ref[:, val_cols(f)], preferred_element_type=F32)

    def gate_act(f, u_ref):
        pg = jnp.where(seq_start, 0.0, carry_g_ref[f])
        pv = jnp.where(seq_start, 0.0, carry_v_ref[f])
        cwg = cw_ref[:, gate_cols(f)]
        cwv = cw_ref[:, val_cols(f)]
        for rs in row_blocks:
            ug = u_ref[rs, :tf]
            uv = u_ref[rs, tf:]
            gate = conv(ug, pg, cwg)
            val = conv(uv, pv, cwv)
            pg = ug[tr - SUBLANES:, :]
            pv = uv[tr - SUBLANES:, :]
            act_ref[rs, gate_cols(f)] = (gate * _sigmoid(gate) * val).astype(BF16)
        carry_g_ref[f] = pg
        carry_v_ref[f] = pv

    up(0, ua_ref)

    def body(i, carry):
        f = 2 * i
        up(f + 1, ub_ref)
        gate_act(f, ua_ref)
        up(f + 2, ua_ref)
        gate_act(f + 1, ub_ref)
        return carry

    lax.fori_loop(0, (nf - 1) // 2, body, 0, unroll=True)
    gate_act(nf - 1, ua_ref)

    for rs in [slice(r * (tm // ND_FFN), (r + 1) * (tm // ND_FFN)) for r in range(ND_FFN)]:
        y = jnp.dot(act_ref[rs, :], wd_ref[...], preferred_element_type=F32)
        out_ref[rs, :] = x_ref[rs, :] + _rms(y, gpost_ref[...])


def _ffn(h2, x1, wu, wd, cw, gpost, seq, l):
    T = h2.shape[0]
    tm = min(TM_FFN, seq)
    tf = TF_FFN
    nf = D_FF // tf
    assert nf % 2 == 1 and nf * tf == D_FF
    row = lambda i: (i, 0)
    layer = lambda i: (l, 0, 0)
    return pl.pallas_call(
        functools.partial(_ffn_kernel, nf=nf, tf=tf, tiles_per_seq=seq // tm),
        grid=(T // tm,),
        in_specs=[
            pl.BlockSpec((tm, D_MODEL), row),
            pl.BlockSpec((tm, D_MODEL), row),
            pl.BlockSpec((None, D_MODEL, 2 * D_FF), layer),
            pl.BlockSpec((None, D_FF, D_MODEL), layer),
            pl.BlockSpec((None, FFN_CONV, 2 * D_FF), layer),
            pl.BlockSpec((None, 1, D_MODEL), layer),
        ],
        out_specs=pl.BlockSpec((tm, D_MODEL), row),
        out_shape=jax.ShapeDtypeStruct((T, D_MODEL), F32),
        scratch_shapes=[
            pltpu.VMEM((tm, D_MODEL), BF16),
            pltpu.VMEM((tm, D_FF), BF16),
            pltpu.VMEM((tm, 2 * tf), F32),
            pltpu.VMEM((tm, 2 * tf), F32),
            pltpu.VMEM((nf, SUBLANES, tf), F32),
            pltpu.VMEM((nf, SUBLANES, tf), F32),
        ],
        compiler_params=pltpu.CompilerParams(
            dimension_semantics=("arbitrary",), vmem_limit_bytes=VMEM_LIMIT),
        name="ffn",
    )(h2, x1, wu, wd, cw, gpost)


def _pair_heads(a, axis):
    G = ATTN_HEADS // ATTN_KV_HEADS
    shape = a.shape
    a = a.reshape(shape[:axis] + (ATTN_KV_HEADS, G, ATTN_HEAD_DIM) + shape[axis + 1:])
    a = jnp.swapaxes(a, axis, axis + 1)
    return a.reshape(shape)


def _rope_tables(seq):
    half = ATTN_HEAD_DIM // 2
    reps = LANES // half
    inv = 1.0 / (ROPE_THETA ** (jnp.arange(0, ATTN_HEAD_DIM, 2, dtype=F32) / ATTN_HEAD_DIM))
    pos = jnp.arange(seq, dtype=F32).reshape(seq // reps, reps, 1)
    ang = (pos * inv[None, None, :]).reshape(seq // reps, LANES)
    cos, sin = lax.optimization_barrier((jnp.cos(ang), jnp.sin(ang)))
    return cos.reshape(seq, half), sin.reshape(seq, half)


def kernel(x, g_pre_mix, w_in, qk_conv_w, qk_conv_b, gate_bias, mh_norm_g, attn_sinks, w_out,
           g_post_mix, g_pre_ffn, w_up, ffn_conv_w, w_down, g_post_ffn):
    batch, seq, _ = x.shape
    depth = w_in.shape[0]
    T = batch * seq
    cos_t, sin_t = _rope_tables(seq)

    n_gate = 2 * MLSTM_HEADS
    o_g = 2 * MLSTM_QK_W + 2 * MLSTM_W
    o_aq = o_g + n_gate
    o_ak = o_aq + ATTN_W
    w_a = jnp.concatenate([
        w_in[:, :, :o_g],
        _pair_heads(w_in[:, :, o_aq:o_ak], 2),
        w_in[:, :, o_ak:],
        w_in[:, :, o_g:o_aq],
        jnp.zeros((depth, D_MODEL, GATE_W - n_gate), w_in.dtype),
    ], axis=2).astype(BF16)
    gb = jnp.concatenate([gate_bias, jnp.zeros((depth, GATE_W - n_gate), F32)], axis=1)[:, None, :]
    w_o = jnp.concatenate([w_out[:, :MLSTM_W], _pair_heads(w_out[:, MLSTM_W:], 1)], axis=1).astype(BF16)
    w_u = w_up.astype(BF16)
    w_d = w_down.astype(BF16)
    vec = lambda p: p[:, None, :]

    x2 = x.reshape(T, D_MODEL)
    for l in range(depth):
        x1, h2 = _mixer(attn_sinks, x2, vec(g_pre_mix), w_a, cos_t, sin_t, qk_conv_w, vec(qk_conv_b), gb,
                        vec(mh_norm_g), w_o, vec(g_post_mix), vec(g_pre_ffn), batch, seq, l)
        x2 = _ffn(h2, x1, w_u, w_d, ffn_conv_w, vec(g_post_ffn), seq, l)
    return x2.reshape(batch, seq, D_MODEL)
```
